```python
import jax, jax.numpy as jnp
from jax import lax
import numpy as np

D_MODEL = 1024
BATCH = 2
SEQ = 8192
DEPTH = 1

D_CONV = D_MODEL // 2
CONV_W = 3
D_RWKV = D_MODEL // 2
HEAD_SIZE = 64
N_HEADS = D_RWKV // HEAD_SIZE
DECAY_LORA = 64
ICLR_LORA = 64
GATE_LORA = 128
GN_EPS = 64e-5
D_FF = 2816
RMS_EPS = 1e-6

COLS_A = 3 * D_CONV
COLS_B = 3 * D_RWKV + DECAY_LORA + ICLR_LORA + GATE_LORA
COLS_G = 2 * D_MODEL
COLS_IN = COLS_A + COLS_B + COLS_G
SPLIT_B = [D_RWKV, 2 * D_RWKV, 3 * D_RWKV, 3 * D_RWKV + DECAY_LORA, 3 * D_RWKV + DECAY_LORA + ICLR_LORA]

kernel_name = "hybrid_shortconv_rwkv7_macaron"


def rms_norm(x, g):
    xf = x.astype(jnp.float32)
    y = xf * lax.rsqrt(jnp.mean(xf * xf, axis=-1, keepdims=True) + RMS_EPS)
    return (y * g.astype(jnp.float32)).astype(x.dtype)


def swiglu(x, w_gate, w_up, w_down):
    return (jax.nn.silu(x @ w_gate) * (x @ w_up)) @ w_down


def token_shift(p):
    return jnp.pad(p, ((0, 0), (1, 0), (0, 0)))[:, :-1]


def causal_dwconv(u, w):
    return lax.conv_general_dilated(
        u, w[:, None, :].astype(u.dtype), window_strides=(1,),
        padding=[(CONV_W - 1, 0)], dimension_numbers=("NWC", "WIO", "NWC"),
        feature_group_count=u.shape[-1])


def wkv7_scan(r, w, k, v, a, b):
    bsz, _, h, n = r.shape

    def step(S, inp):
        r_t, w_t, k_t, v_t, a_t, b_t = inp
        sa = jnp.einsum("bhvk,bhk->bhv", S, a_t)
        S = S * w_t[:, :, None, :] + sa[..., None] * b_t[:, :, None, :] + v_t[..., None] * k_t[:, :, None, :]
        y = jnp.einsum("bhvk,bhk->bhv", S, r_t)
        return S, y

    xs = tuple(jnp.moveaxis(t, 1, 0) for t in (r, w, k, v, a, b))
    S0 = jnp.zeros((bsz, h, n, n), jnp.float32)
    _, ys = lax.scan(step, S0, xs)
    return jnp.moveaxis(ys, 0, 1)


def setup_inputs(seed: int = 0) -> dict:
    key = jax.random.key(seed)
    ks = iter(jax.random.split(key, 40))
    f32 = jnp.float32

    def nrm(shape, scale):
        return jax.random.normal(next(ks), shape, f32) * scale

    def gain(shape):
        return 1.0 + 0.01 * jax.random.normal(next(ks), shape, f32)

    L = DEPTH
    return {
        "x": jax.random.normal(next(ks), (BATCH, SEQ, D_MODEL), f32),
        "ffn1_norm": gain((L, D_MODEL)),
        "ffn1_w_gate": nrm((L, D_MODEL, D_FF), D_MODEL ** -0.5),
        "ffn1_w_up": nrm((L, D_MODEL, D_FF), D_MODEL ** -0.5),
        "ffn1_w_down": nrm((L, D_FF, D_MODEL), D_FF ** -0.5),
        "mix_norm": gain((L, D_MODEL)),
        "w_in": nrm((L, D_MODEL, COLS_IN), D_MODEL ** -0.5),
        "conv_w": nrm((L, CONV_W, D_CONV), CONV_W ** -0.5),
        "w_out_a": nrm((L, D_CONV, D_MODEL), D_CONV ** -0.5),
        "mu_b": jax.random.uniform(next(ks), (L, COLS_B), f32),
        "w0": nrm((L, D_RWKV), 0.5) - 0.5,
        "w_decay_up": nrm((L, DECAY_LORA, D_RWKV), 0.5 * DECAY_LORA ** -0.5),
        "a0": nrm((L, D_RWKV), 0.1),
        "w_iclr_up": nrm((L, ICLR_LORA, D_RWKV), ICLR_LORA ** -0.5),
        "w_gate_up": nrm((L, GATE_LORA, D_RWKV), GATE_LORA ** -0.5),
        "k_k": 0.85 + nrm((L, D_RWKV), 0.05),
        "k_a": 1.0 + nrm((L, D_RWKV), 0.05),
        "r_k": nrm((L, N_HEADS, HEAD_SIZE), 0.1),
        "ln_x_w": gain((L, D_RWKV)),
        "ln_x_b": nrm((L, D_RWKV), 0.01),
        "w_out_b": nrm((L, D_RWKV, D_MODEL), D_RWKV ** -0.5),
        "w_o": nrm((L, D_MODEL, D_MODEL), D_MODEL ** -0.5),
        "ffn2_norm": gain((L, D_MODEL)),
        "ffn2_w_gate": nrm((L, D_MODEL, D_FF), D_MODEL ** -0.5),
        "ffn2_w_up": nrm((L, D_MODEL, D_FF), D_MODEL ** -0.5),
        "ffn2_w_down": nrm((L, D_FF, D_MODEL), D_FF ** -0.5),
        "final_norm": gain((D_MODEL,)),
    }


def token_mixer(h, w_in, conv_w, w_out_a, mu_b, w0, w_decay_up, a0, w_iclr_up,
                w_gate_up, k_k, k_a, r_k, ln_x_w, ln_x_b, w_out_b, w_o):
    bsz, t, _ = h.shape
    f32 = jnp.float32
    p = h @ w_in
    pa, pb, pg = jnp.split(p, [COLS_A, COLS_A + COLS_B], axis=-1)

    b_gate, c_gate, u = jnp.split(pa, 3, axis=-1)
    y_a = (b_gate * causal_dwconv(c_gate * u, conv_w)) @ w_out_a

    pb = pb + (token_shift(pb) - pb) * mu_b
    r, k, v, xw, xa, xg = jnp.split(pb, SPLIT_B, axis=-1)
    w_log = -jax.nn.softplus(-(w0 + jnp.tanh(xw) @ w_decay_up)) - 0.5
    decay = jnp.exp(-jnp.exp(w_log.astype(f32)))
    iclr = jax.nn.sigmoid(a0 + xa @ w_iclr_up)
    g = jax.nn.sigmoid(xg) @ w_gate_up

    def heads(z):
        return z.astype(f32).reshape(bsz, t, N_HEADS, HEAD_SIZE)

    kk = heads(k * k_k)
    kk = kk / jnp.maximum(jnp.sqrt(jnp.sum(kk * kk, axis=-1, keepdims=True)), 1e-12)
    k = k * (1.0 + (iclr - 1.0) * k_a)
    rh, kh, vh, ah = heads(r), heads(k), heads(v), heads(iclr)
    y = wkv7_scan(rh, heads(decay), kh, vh, -kk, kk * ah)

    mu = jnp.mean(y, axis=-1, keepdims=True)
    var = jnp.mean(jnp.square(y - mu), axis=-1, keepdims=True)
    y = (y - mu) * lax.rsqrt(var + GN_EPS)
    y = (y * ln_x_w.astype(f32).reshape(N_HEADS, HEAD_SIZE)
         + ln_x_b.astype(f32).reshape(N_HEADS, HEAD_SIZE))
    y = y + jnp.sum(rh * kh * r_k.astype(f32), axis=-1, keepdims=True) * vh
    y = y.reshape(bsz, t, D_RWKV).astype(h.dtype)
    y_b = (y * g) @ w_out_b

    g_a, g_b = jnp.split(pg, 2, axis=-1)
    merged = jax.nn.sigmoid(g_a) * y_a + jax.nn.sigmoid(g_b) * y_b
    return merged @ w_o


def reference(x, ffn1_norm, ffn1_w_gate, ffn1_w_up, ffn1_w_down, mix_norm, w_in,
              conv_w, w_out_a, mu_b, w0, w_decay_up, a0, w_iclr_up, w_gate_up,
              k_k, k_a, r_k, ln_x_w, ln_x_b, w_out_b, w_o, ffn2_norm,
              ffn2_w_gate, ffn2_w_up, ffn2_w_down, final_norm):
    for l in range(DEPTH):
        x = x + 0.5 * swiglu(rms_norm(x, ffn1_norm[l]), ffn1_w_gate[l], ffn1_w_up[l], ffn1_w_down[l])
        x = x + token_mixer(rms_norm(x, mix_norm[l]), w_in[l], conv_w[l], w_out_a[l], mu_b[l],
                            w0[l], w_decay_up[l], a0[l], w_iclr_up[l], w_gate_up[l],
                            k_k[l], k_a[l], r_k[l], ln_x_w[l], ln_x_b[l], w_out_b[l], w_o[l])
        x = x + 0.5 * swiglu(rms_norm(x, ffn2_norm[l]), ffn2_w_gate[l], ffn2_w_up[l], ffn2_w_down[l])
    return rms_norm(x, final_norm)
```

```python
import functools

import jax
import jax.numpy as jnp
from jax import lax
from jax.experimental import pallas as pl
from jax.experimental.pallas import tpu as pltpu

F32 = jnp.float32
BF = jnp.bfloat16

D_MODEL = 1024
D_CONV = 512
D_RWKV = 512
HEAD = 64
D_FF = 2816
COLS_A = 3 * D_CONV
COLS_B = 3 * D_RWKV + 64 + 64 + 128
OFF_B = COLS_A
OFF_GA = COLS_A + COLS_B
OFF_GB = OFF_GA + D_MODEL
RMS_EPS = 1e-6
GN_EPS = 64e-5

CHUNK = 64
PAIR = 2 * HEAD
N_PAIR = D_RWKV // PAIR
HALO = 8
TM_FFN = 512
TF_FFN = 256
TM_MIX = 512
VMEM_LIMIT = 56 * 1024 * 1024


def _dot(a, b):
    return jnp.dot(a.astype(BF), b.astype(BF), preferred_element_type=F32)


def _dot_nt(a, b):
    return lax.dot_general(a.astype(BF), b.astype(BF), (((1,), (1,)), ((), ())),
                           preferred_element_type=F32)


def _split2(x):
    hi = x.astype(BF)
    lo = (x - hi.astype(F32)).astype(BF)
    return hi, lo


def _dot3(a, b):
    ah, al = _split2(a)
    bh, bl = _split2(b)
    return (jnp.dot(ah, bh, preferred_element_type=F32)
            + (jnp.dot(ah, bl, preferred_element_type=F32)
               + jnp.dot(al, bh, preferred_element_type=F32)))


def _dot3_nt(a, b):
    dn = (((1,), (1,)), ((), ()))
    ah, al = _split2(a)
    bh, bl = _split2(b)
    return (lax.dot_general(ah, bh, dn, preferred_element_type=F32)
            + (lax.dot_general(ah, bl, dn, preferred_element_type=F32)
               + lax.dot_general(al, bh, dn, preferred_element_type=F32)))


def _dot_exact_rhs(a, b_bf):
    ah, al = _split2(a)
    return (jnp.dot(ah, b_bf, preferred_element_type=F32)
            + jnp.dot(al, b_bf, preferred_element_type=F32))


def _rms_norm(x, gain):
    return x * lax.rsqrt(jnp.mean(x * x, axis=-1, keepdims=True) + RMS_EPS) * gain


def _ffn_body(*refs, final_norm):
    if final_norm:
        x_ref, gain_ref, wg_ref, wu_ref, wd_ref, fn_ref, o_ref, act_ref = refs
    else:
        x_ref, gain_ref, wg_ref, wu_ref, wd_ref, o_ref, act_ref = refs
    x = x_ref[...]
    h = _rms_norm(x, gain_ref[...]).astype(BF)
    for c in range(D_FF // TF_FFN):
        sl = slice(c * TF_FFN, (c + 1) * TF_FFN)
        g = jnp.dot(h, wg_ref[:, sl], preferred_element_type=F32)
        u = jnp.dot(h, wu_ref[:, sl], preferred_element_type=F32)
        act_ref[:, sl] = (g * jax.nn.sigmoid(g) * u).astype(BF)
    y = x + 0.5 * jnp.dot(act_ref[...], wd_ref[...], preferred_element_type=F32)
    if final_norm:
        y = _rms_norm(y, fn_ref[...])
    o_ref[...] = y


def _const_spec(shape):
    return pl.BlockSpec(shape, lambda *_: (0,) * len(shape), pipeline_mode=pl.Buffered(1))


def _ffn(x, gain, wg, wu, wd, final_gain=None):
    m = x.shape[0]
    final_norm = final_gain is not None
    in_specs = [
        pl.BlockSpec((TM_FFN, D_MODEL), lambda i: (i, 0)),
        _const_spec((1, D_MODEL)),
        _const_spec((D_MODEL, D_FF)),
        _const_spec((D_MODEL, D_FF)),
        _const_spec((D_FF, D_MODEL)),
    ]
    args = [x, gain, wg, wu, wd]
    if final_norm:
        in_specs.append(_const_spec((1, D_MODEL)))
        args.append(final_gain)
    return pl.pallas_call(
        functools.partial(_ffn_body, final_norm=final_norm),
        grid=(m // TM_FFN,),
        in_specs=in_specs,
        out_specs=pl.BlockSpec((TM_FFN, D_MODEL), lambda i: (i, 0)),
        out_shape=jax.ShapeDtypeStruct((m, D_MODEL), F32),
        scratch_shapes=[pltpu.VMEM((TM_FFN, D_FF), BF)],
        compiler_params=pltpu.CompilerParams(
            dimension_semantics=("arbitrary",), vmem_limit_bytes=VMEM_LIMIT),
        name="ffn_final" if final_norm else "ffn",
    )(*args)


def _wkv_pair_chunk(r, k, v, a, b, cs, csp, tot, s_stack):
    c = CHUNK
    at = a * jnp.exp(csp)
    rt = r * jnp.exp(cs)
    e_n = jnp.exp(-cs)
    bt = b * e_n
    kt = k * e_n
    e_t = jnp.exp(tot - cs)
    bh = b * e_t
    kh = k * e_t

    lane = lax.broadcasted_iota(jnp.int32, (c, PAIR), 1)
    head0 = lane < HEAD
    zero = jnp.zeros_like(at)
    ar0 = jnp.concatenate([jnp.where(head0, at, zero), jnp.where(head0, rt, zero)], axis=0)
    ar1 = jnp.concatenate([jnp.where(head0, zero, at), jnp.where(head0, zero, rt)], axis=0)
    bk = jnp.concatenate([bt, kt], axis=0)
    kb = jnp.concatenate([kt, bt], axis=0)
    g0 = _dot3_nt(ar0, bk)
    g1 = _dot3_nt(ar1, kb)

    row2 = lax.broadcasted_iota(jnp.int32, (PAIR, PAIR), 0)
    lane2 = lax.broadcasted_iota(jnp.int32, (PAIR, PAIR), 1)
    t_idx = row2 % c
    s_idx = lane2 % c
    blockdiag = (row2 // c) == (lane2 // c)
    zero2 = jnp.zeros((PAIR, PAIR), F32)
    aa = jnp.where(s_idx < t_idx, jnp.concatenate([g0[:c], g1[:c]], axis=0), zero2)
    arr = jnp.where(s_idx <= t_idx, jnp.concatenate([g0[c:], g1[c:]], axis=0), zero2)
    l_bd = jnp.where(blockdiag, aa, zero2)
    ak_ad = jnp.where(blockdiag, zero2, aa)

    eye = jnp.where(row2 == lane2, 1.0, 0.0).astype(F32)
    p = eye + l_bd
    q = _dot3(l_bd, l_bd)
    for _ in range(4):
        pq = _dot3(jnp.concatenate([p, q], axis=0), q)
        p = p + pq[:PAIR]
        q = pq[PAIR:]
    t_inv = p + _dot3(p, q)

    xs = _dot3_nt(jnp.concatenate([ar0, ar1], axis=0), s_stack)
    x_bd = jnp.concatenate([xs[0:c], xs[2 * c:3 * c]], axis=0)
    rs_bd = jnp.concatenate([xs[c:2 * c], xs[3 * c:4 * c]], axis=0)

    v_swap = jnp.concatenate([jnp.where(head0, zero, v), jnp.where(head0, v, zero)], axis=0)
    u_bd = _dot3(t_inv, x_bd + _dot3(ak_ad, v_swap))
    y_bd = _dot3(arr, u_bd + v_swap) + rs_bd
    y = jnp.where(head0, y_bd[:c], y_bd[c:])

    u_pair = u_bd[:c] + u_bd[c:]
    uv = jnp.concatenate([u_pair, v], axis=0)
    bkh = jnp.concatenate([bh, kh], axis=0)
    upd = _dot3(uv.T, bkh)
    s_new = jnp.where(blockdiag, s_stack * jnp.exp(tot) + upd, zero2)
    return y, s_new


def _mixer_body(x_ref, gain_ref, win_ref, convw_ref, wouta_ref, mu_ref, w0_ref, wdec_ref,
                a0_ref, wicl_ref, wgate_ref, kk_ref, ka_ref, rk_ref, lnw_ref, lnb_ref,
                woutb_ref, wo_ref, bd_ref, tril_ref,
                o_ref,
                cu_buf, pb_buf, r_s, k_s, v_s, a_s, b_s, lw_s, g_s, bon_s, y_s, m_s, st_s):
    tm = TM_MIX

    @pl.when(pl.program_id(1) == 0)
    def _():
        cu_buf[0:HALO, :] = jnp.zeros((HALO, D_CONV), F32)
        pb_buf[0:HALO, :] = jnp.zeros((HALO, COLS_B), F32)
        st_s[...] = jnp.zeros_like(st_s)

    x = x_ref[...]
    h = _rms_norm(x, gain_ref[...]).astype(BF)
    bd = bd_ref[...]

    def head_sum(z):
        return _dot_exact_rhs(z, bd)

    pa = jnp.dot(h, win_ref[:, 0:COLS_A], preferred_element_type=F32)
    cu = pa[:, D_CONV:2 * D_CONV] * pa[:, 2 * D_CONV:3 * D_CONV]
    cu_buf[HALO:HALO + tm, :] = cu
    cw = convw_ref[...]
    conv = (cw[2:3] * cu + cw[1:2] * cu_buf[pl.ds(HALO - 1, tm), :]
            + cw[0:1] * cu_buf[pl.ds(HALO - 2, tm), :])
    cu_buf[0:HALO, :] = cu_buf[tm:tm + HALO, :]
    ya = _dot(pa[:, 0:D_CONV] * conv, wouta_ref[...])
    ga = jnp.dot(h, win_ref[:, OFF_GA:OFF_GB], preferred_element_type=F32)
    m_s[...] = jax.nn.sigmoid(ga) * ya

    pb = jnp.dot(h, win_ref[:, OFF_B:OFF_GA], preferred_element_type=F32)
    pb_buf[HALO:HALO + tm, :] = pb
    prev = pb_buf[pl.ds(HALO - 1, tm), :]
    pb_buf[0:HALO, :] = pb_buf[tm:tm + HALO, :]
    pbm = pb + (prev - pb) * mu_ref[...]
    r = pbm[:, 0:D_RWKV]
    k = pbm[:, D_RWKV:2 * D_RWKV]
    v = pbm[:, 2 * D_RWKV:3 * D_RWKV]
    xwa = pbm[:, 3 * D_RWKV:3 * D_RWKV + 128]
    xg = pbm[:, 3 * D_RWKV + 128:COLS_B]

    z = w0_ref[...] + _dot3(jnp.tanh(xwa), wdec_ref[...])
    nz = -z
    softplus = jnp.maximum(nz, 0.0) + jnp.log1p(jnp.exp(-jnp.abs(nz)))
    w_log = -softplus - 0.5
    lw_s[...] = -jnp.exp(w_log)
    iclr = jax.nn.sigmoid(a0_ref[...] + _dot3(xwa, wicl_ref[...]))
    g_s[...] = _dot(jax.nn.sigmoid(xg), wgate_ref[...])

    kk = k * kk_ref[...]
    kk = kk / jnp.maximum(jnp.sqrt(head_sum(kk * kk)), 1e-12)
    k2 = k * (1.0 + (iclr - 1.0) * ka_ref[...])
    r_s[...] = r
    k_s[...] = k2
    v_s[...] = v
    a_s[...] = -kk
    b_s[...] = kk * iclr
    bon_s[...] = head_sum(r * k2 * rk_ref[...]) * v

    tril = tril_ref[...]

    def chunk_body(ci, carry):
        rows = pl.ds(pl.multiple_of(ci * CHUNK, CHUNK), CHUNK)
        lw = lw_s[rows, :]
        h1 = lw.astype(BF)
        r1 = lw - h1.astype(F32)
        h2 = r1.astype(BF)
        h3 = (r1 - h2.astype(F32)).astype(BF)
        cs = (jnp.dot(tril, h1, preferred_element_type=F32)
              + (jnp.dot(tril, h2, preferred_element_type=F32)
                 + jnp.dot(tril, h3, preferred_element_type=F32)))
        csp = cs - lw
        tot = cs[CHUNK - 1:CHUNK, :]
        for j in range(N_PAIR):
            sl = slice(j * PAIR, (j + 1) * PAIR)
            y, s_new = _wkv_pair_chunk(
                r_s[rows, sl], k_s[rows, sl], v_s[rows, sl], a_s[rows, sl], b_s[rows, sl],
                cs[:, sl], csp[:, sl], tot[:, sl], st_s[j])
            y_s[rows, sl] = y
            st_s[j] = s_new
        return carry

    lax.fori_loop(0, tm // CHUNK, chunk_body, 0)

    y = y_s[...]
    mean = head_sum(y) * (1.0 / HEAD)
    d = y - mean
    var = head_sum(d * d) * (1.0 / HEAD)
    yn = d * lax.rsqrt(var + GN_EPS) * lnw_ref[...] + lnb_ref[...] + bon_s[...]
    yb = _dot(yn * g_s[...], woutb_ref[...])
    gb = jnp.dot(h, win_ref[:, OFF_GB:OFF_GB + D_MODEL], preferred_element_type=F32)
    merged = m_s[...] + jax.nn.sigmoid(gb) * yb
    o_ref[...] = x + _dot(merged, wo_ref[...])


def _mixer(x, n_batch, gain, win, convw, wouta, mu, w0, wdec, a0, wicl, wgate, kk, ka, rk,
           lnw, lnb, woutb, wo):
    m = x.shape[0]
    tiles = m // n_batch // TM_MIX
    head_of = jnp.arange(D_RWKV, dtype=jnp.int32) // HEAD
    bd = (head_of[:, None] == head_of[None, :]).astype(BF)
    idx = jnp.arange(CHUNK, dtype=jnp.int32)
    tril = (idx[None, :] <= idx[:, None]).astype(BF)
    consts = [gain, win, convw, wouta, mu, w0, wdec, a0, wicl, wgate, kk, ka, rk, lnw, lnb,
              woutb, wo, bd, tril]
    tok_spec = pl.BlockSpec((TM_MIX, D_MODEL), lambda bi, ti: (bi * tiles + ti, 0))
    vec = lambda n: pltpu.VMEM((TM_MIX, n), F32)
    return pl.pallas_call(
        _mixer_body,
        grid=(n_batch, tiles),
        in_specs=[tok_spec] + [_const_spec(c.shape) for c in consts],
        out_specs=tok_spec,
        out_shape=jax.ShapeDtypeStruct((m, D_MODEL), F32),
        scratch_shapes=[
            pltpu.VMEM((TM_MIX + HALO, D_CONV), F32),
            pltpu.VMEM((TM_MIX + HALO, COLS_B), F32),
            vec(D_RWKV), vec(D_RWKV), vec(D_RWKV), vec(D_RWKV), vec(D_RWKV), vec(D_RWKV),
            vec(D_RWKV), vec(D_RWKV), vec(D_RWKV), vec(D_MODEL),
            pltpu.VMEM((N_PAIR, PAIR, PAIR), F32),
        ],
        compiler_params=pltpu.CompilerParams(
            dimension_semantics=("arbitrary", "arbitrary"), vmem_limit_bytes=VMEM_LIMIT),
        name="mixer",
    )(x, *consts)


def kernel(x, ffn1_norm, ffn1_w_gate, ffn1_w_up, ffn1_w_down, mix_norm, w_in, conv_w, w_out_a, mu_b, w0, w_decay_up, a0, w_iclr_up, w_gate_up, k_k, k_a, r_k, ln_x_w, ln_x_b, w_out_b, w_o, ffn2_norm, ffn2_w_gate, ffn2_w_up, ffn2_w_down, final_norm):
    n_batch, seq, d = x.shape
    assert d == D_MODEL and seq % TM_MIX == 0 and (n_batch * seq) % TM_FFN == 0
    assert ffn1_norm.shape[0] == 1, "single layer"
    row = lambda t: t.reshape(1, -1).astype(F32)
    bf = lambda t: t.astype(BF)
    xf = x.reshape(n_batch * seq, d)

    x1 = _ffn(xf, row(ffn1_norm[0]), bf(ffn1_w_gate[0]), bf(ffn1_w_up[0]), bf(ffn1_w_down[0]))

    zeros_lora = jnp.zeros((64, D_RWKV), F32)
    wdec = jnp.concatenate([w_decay_up[0], zeros_lora], axis=0)
    wicl = jnp.concatenate([zeros_lora, w_iclr_up[0]], axis=0)
    x2 = _mixer(x1, n_batch, row(mix_norm[0]), bf(w_in[0]), conv_w[0].astype(F32),
                bf(w_out_a[0]), row(mu_b[0]), row(w0[0]), wdec, row(a0[0]), wicl,
                bf(w_gate_up[0]), row(k_k[0]), row(k_a[0]), row(r_k[0]), row(ln_x_w[0]),
                row(ln_x_b[0]), bf(w_out_b[0]), bf(w_o[0]))

    out = _ffn(x2, row(ffn2_norm[0]), bf(ffn2_w_gate[0]), bf(ffn2_w_up[0]), bf(ffn2_w_down[0]),
               final_gain=row(final_norm))
    return out.reshape(n_batch, seq, d)
```

```python
import functools

import jax
import jax.numpy as jnp
from jax import lax
from jax.experimental import pallas as pl
from jax.experimental.pallas import tpu as pltpu

F32 = jnp.float32
BF = jnp.bfloat16

D_MODEL = 1024
D_CONV = 512
D_RWKV = 512
HEAD = 64
D_FF = 2816
COLS_A = 3 * D_CONV
COLS_B = 3 * D_RWKV + 64 + 64 + 128
OFF_B = COLS_A
OFF_GA = COLS_A + COLS_B
OFF_GB = OFF_GA + D_MODEL
RMS_EPS = 1e-6
GN_EPS = 64e-5

CHUNK = 64
PAIR = 2 * HEAD
N_PAIR = D_RWKV // PAIR
HALO = 8
TM_FFN = 512
TF_FFN = 256
TM_MIX = 512
VMEM_LIMIT = 56 * 1024 * 1024


def _dot(a, b):
    return jnp.dot(a.astype(BF), b.astype(BF), preferred_element_type=F32)


def _dot_nt(a, b):
    return lax.dot_general(a.astype(BF), b.astype(BF), (((1,), (1,)), ((), ())),
                           preferred_element_type=F32)


def _split2(x):
    hi = x.astype(BF)
    lo = (x - hi.astype(F32)).astype(BF)
    return hi, lo


def _dot3(a, b):
    ah, al = _split2(a)
    bh, bl = _split2(b)
    return (jnp.dot(ah, bh, preferred_element_type=F32)
            + (jnp.dot(ah, bl, preferred_element_type=F32)
               + jnp.dot(al, bh, preferred_element_type=F32)))


def _dot_exact_rhs(a, b_bf):
    ah, al = _split2(a)
    return (jnp.dot(ah, b_bf, preferred_element_type=F32)
            + jnp.dot(al, b_bf, preferred_element_type=F32))


def _rms_norm(x, gain):
    return x * lax.rsqrt(jnp.mean(x * x, axis=-1, keepdims=True) + RMS_EPS) * gain


def _ffn_body(*refs, final_norm):
    if final_norm:
        x_ref, gain_ref, wg_ref, wu_ref, wd_ref, fn_ref, o_ref, act_ref = refs
    else:
        x_ref, gain_ref, wg_ref, wu_ref, wd_ref, o_ref, act_ref = refs
    x = x_ref[...]
    h = _rms_norm(x, gain_ref[...]).astype(BF)
    for c in range(D_FF // TF_FFN):
        sl = slice(c * TF_FFN, (c + 1) * TF_FFN)
        g = jnp.dot(h, wg_ref[:, sl], preferred_element_type=F32)
        u = jnp.dot(h, wu_ref[:, sl], preferred_element_type=F32)
        act_ref[:, sl] = (g * jax.nn.sigmoid(g) * u).astype(BF)
    y = x + 0.5 * jnp.dot(act_ref[...], wd_ref[...], preferred_element_type=F32)
    if final_norm:
        y = _rms_norm(y, fn_ref[...])
    o_ref[...] = y


def _const_spec(shape):
    return pl.BlockSpec(shape, lambda *_: (0,) * len(shape), pipeline_mode=pl.Buffered(1))


def _ffn(x, gain, wg, wu, wd, final_gain=None):
    m = x.shape[0]
    final_norm = final_gain is not None
    in_specs = [
        pl.BlockSpec((TM_FFN, D_MODEL), lambda i: (i, 0)),
        _const_spec((1, D_MODEL)),
        _const_spec((D_MODEL, D_FF)),
        _const_spec((D_MODEL, D_FF)),
        _const_spec((D_FF, D_MODEL)),
    ]
    args = [x, gain, wg, wu, wd]
    if final_norm:
        in_specs.append(_const_spec((1, D_MODEL)))
        args.append(final_gain)
    return pl.pallas_call(
        functools.partial(_ffn_body, final_norm=final_norm),
        grid=(m // TM_FFN,),
        in_specs=in_specs,
        out_specs=pl.BlockSpec((TM_FFN, D_MODEL), lambda i: (i, 0)),
        out_shape=jax.ShapeDtypeStruct((m, D_MODEL), F32),
        scratch_shapes=[pltpu.VMEM((TM_FFN, D_FF), BF)],
        compiler_params=pltpu.CompilerParams(
            dimension_semantics=("arbitrary",), vmem_limit_bytes=VMEM_LIMIT),
        name="ffn_final" if final_norm else "ffn",
    )(*args)


def _pair_index():
    row = lax.broadcasted_iota(jnp.int32, (PAIR, PAIR), 0)
    lane = lax.broadcasted_iota(jnp.int32, (PAIR, PAIR), 1)
    return row, lane


def _tri_inverse(l_bds):
    row, lane = _pair_index()
    zero = jnp.zeros((PAIR, PAIR), F32)
    eye = jnp.where(row == lane, 1.0, 0.0).astype(F32)

    def off(m):
        return ((row // (2 * m)) == (lane // (2 * m))) & ((row % (2 * m)) >= m) & ((lane % (2 * m)) < m)

    ts = [eye + jnp.where(off(1), l, zero) for l in l_bds]
    m = 2
    while m < CHUNK:
        mask = off(m)
        tl = [_dot(t, jnp.where(mask, l, zero)) for t, l in zip(ts, l_bds)]
        ts = [t + _dot(x, t) for t, x in zip(ts, tl)]
        m *= 2
    return ts


def _wkv_tables(r, k, v, a, b, cs, csp, tot):
    c = CHUNK
    at = a * jnp.exp(csp)
    rt = r * jnp.exp(cs)
    e_n = jnp.exp(-cs)
    bt = b * e_n
    kt = k * e_n
    e_t = jnp.exp(tot - cs)
    bh = b * e_t
    kh = k * e_t

    head0 = lax.broadcasted_iota(jnp.int32, (c, PAIR), 1) < HEAD
    row, lane = _pair_index()
    t_idx = row % c
    s_idx = lane % c
    blockdiag = (row // c) == (lane // c)
    zero = jnp.zeros((c, PAIR), F32)
    zero2 = jnp.zeros((PAIR, PAIR), F32)

    arstk, arr, l_bd, ak_ad, bkh, v_swap = [], [], [], [], [], []
    for j in range(N_PAIR):
        sl = slice(j * PAIR, (j + 1) * PAIR)
        ar0 = jnp.concatenate([jnp.where(head0, at[:, sl], zero), jnp.where(head0, rt[:, sl], zero)], axis=0)
        ar1 = jnp.concatenate([jnp.where(head0, zero, at[:, sl]), jnp.where(head0, zero, rt[:, sl])], axis=0)
        g0 = _dot_nt(ar0, jnp.concatenate([bt[:, sl], kt[:, sl]], axis=0))
        g1 = _dot_nt(ar1, jnp.concatenate([kt[:, sl], bt[:, sl]], axis=0))
        aa = jnp.where(s_idx < t_idx, jnp.concatenate([g0[:c], g1[:c]], axis=0), zero2)
        arr.append(jnp.where(s_idx <= t_idx, jnp.concatenate([g0[c:], g1[c:]], axis=0), zero2))
        l_bd.append(jnp.where(blockdiag, aa, zero2))
        ak_ad.append(jnp.where(blockdiag, zero2, aa))
        arstk.append(jnp.concatenate([ar0, ar1], axis=0))
        bkh.append(jnp.concatenate([bh[:, sl], kh[:, sl]], axis=0))
        v_swap.append(jnp.concatenate([jnp.where(head0, zero, v[:, sl]),
                                       jnp.where(head0, v[:, sl], zero)], axis=0))
    t_inv = _tri_inverse(l_bd)
    akv = [_dot(x, y) for x, y in zip(ak_ad, v_swap)]
    return arstk, t_inv, arr, akv, bkh, v_swap


def _wkv_apply(tables, v, e_tot, states):
    c = CHUNK
    arstk, t_inv, arr, akv, bkh, v_swap = tables
    head0 = lax.broadcasted_iota(jnp.int32, (c, PAIR), 1) < HEAD
    row, lane = _pair_index()
    blockdiag = (row // c) == (lane // c)
    zero2 = jnp.zeros((PAIR, PAIR), F32)
    xs = [_dot_nt(x, s) for x, s in zip(arstk, states)]
    x_bd = [jnp.concatenate([z[0:c], z[2 * c:3 * c]], axis=0) for z in xs]
    rs_bd = [jnp.concatenate([z[c:2 * c], z[3 * c:4 * c]], axis=0) for z in xs]
    u_bd = [_dot(t, x + w) for t, x, w in zip(t_inv, x_bd, akv)]
    y_bd = [_dot(m, u + w) + z for m, u, w, z in zip(arr, u_bd, v_swap, rs_bd)]
    ys = [jnp.where(head0, z[:c], z[c:]) for z in y_bd]
    new_states = []
    for j in range(N_PAIR):
        sl = slice(j * PAIR, (j + 1) * PAIR)
        uv = jnp.concatenate([u_bd[j][:c] + u_bd[j][c:], v[:, sl]], axis=0)
        upd = _dot(uv.T, bkh[j])
        new_states.append(jnp.where(blockdiag, states[j] * e_tot[:, sl] + upd, zero2))
    return jnp.concatenate(ys, axis=1), new_states


def _mixer_body(x_ref, gain_ref, win_ref, convw_ref, wouta_ref, mu_ref, w0_ref, wdec_ref,
                a0_ref, wicl_ref, wgate_ref, kk_ref, ka_ref, rk_ref, lnw_ref, lnb_ref,
                woutb_ref, wo_ref, bd_ref, tril_ref,
                o_ref,
                cu_buf, pb_buf, r_s, k_s, v_s, a_s, b_s, lw_s, g_s, bon_s, y_s, m_s, st_s,
                arstk_s, tinv_s, arr_s, akv_s, bkh_s, vsw_s, etot_s):
    tm = TM_MIX
    table_refs = (arstk_s, tinv_s, arr_s, akv_s, bkh_s, vsw_s)

    @pl.when(pl.program_id(1) == 0)
    def _():
        cu_buf[0:HALO, :] = jnp.zeros((HALO, D_CONV), F32)
        pb_buf[0:HALO, :] = jnp.zeros((HALO, COLS_B), F32)
        st_s[...] = jnp.zeros_like(st_s)

    x = x_ref[...]
    h = _rms_norm(x, gain_ref[...]).astype(BF)
    bd = bd_ref[...]

    def head_sum(z):
        return _dot_exact_rhs(z, bd)

    pa = jnp.dot(h, win_ref[:, 0:COLS_A], preferred_element_type=F32)
    cu = pa[:, D_CONV:2 * D_CONV] * pa[:, 2 * D_CONV:3 * D_CONV]
    cu_buf[HALO:HALO + tm, :] = cu
    cw = convw_ref[...]
    conv = (cw[2:3] * cu + cw[1:2] * cu_buf[pl.ds(HALO - 1, tm), :]
            + cw[0:1] * cu_buf[pl.ds(HALO - 2, tm), :])
    cu_buf[0:HALO, :] = cu_buf[tm:tm + HALO, :]
    ya = _dot(pa[:, 0:D_CONV] * conv, wouta_ref[...])
    ga = jnp.dot(h, win_ref[:, OFF_GA:OFF_GB], preferred_element_type=F32)
    m_s[...] = jax.nn.sigmoid(ga) * ya

    pb = jnp.dot(h, win_ref[:, OFF_B:OFF_GA], preferred_element_type=F32)
    pb_buf[HALO:HALO + tm, :] = pb
    prev = pb_buf[pl.ds(HALO - 1, tm), :]
    pb_buf[0:HALO, :] = pb_buf[tm:tm + HALO, :]
    pbm = pb + (prev - pb) * mu_ref[...]
    r = pbm[:, 0:D_RWKV]
    k = pbm[:, D_RWKV:2 * D_RWKV]
    v = pbm[:, 2 * D_RWKV:3 * D_RWKV]
    xwa = pbm[:, 3 * D_RWKV:3 * D_RWKV + 128]
    xg = pbm[:, 3 * D_RWKV + 128:COLS_B]

    z = w0_ref[...] + _dot3(jnp.tanh(xwa), wdec_ref[...])
    nz = -z
    softplus = jnp.maximum(nz, 0.0) + jnp.log1p(jnp.exp(-jnp.abs(nz)))
    w_log = -softplus - 0.5
    lw_s[...] = -jnp.exp(w_log)
    iclr = jax.nn.sigmoid(a0_ref[...] + _dot3(xwa, wicl_ref[...]))
    g_s[...] = _dot(jax.nn.sigmoid(xg), wgate_ref[...])

    kk = k * kk_ref[...]
    kk = kk / jnp.maximum(jnp.sqrt(head_sum(kk * kk)), 1e-12)
    k2 = k * (1.0 + (iclr - 1.0) * ka_ref[...])
    r_s[...] = r
    k_s[...] = k2
    v_s[...] = v
    a_s[...] = -kk
    b_s[...] = kk * iclr
    bon_s[...] = head_sum(r * k2 * rk_ref[...]) * v

    tril = tril_ref[...]
    n_chunk = tm // CHUNK

    def chunk_rows(ci):
        start = ci * CHUNK
        return pl.ds(start if isinstance(ci, int) else pl.multiple_of(start, CHUNK), CHUNK)

    def make_tables(ci):
        rows = chunk_rows(ci)
        lw = lw_s[rows, :]
        h1 = lw.astype(BF)
        r1 = lw - h1.astype(F32)
        h2 = r1.astype(BF)
        h3 = (r1 - h2.astype(F32)).astype(BF)
        cs = (jnp.dot(tril, h1, preferred_element_type=F32)
              + (jnp.dot(tril, h2, preferred_element_type=F32)
                 + jnp.dot(tril, h3, preferred_element_type=F32)))
        tot = cs[CHUNK - 1:CHUNK, :]
        tables = _wkv_tables(r_s[rows, :], k_s[rows, :], v_s[rows, :], a_s[rows, :],
                             b_s[rows, :], cs, cs - lw, tot)
        return tables, jnp.exp(tot)

    def store_tables(tables, e_tot):
        for ref, vals in zip(table_refs, tables):
            for j in range(N_PAIR):
                ref[j] = vals[j].astype(ref.dtype)
        etot_s[...] = e_tot

    def apply_chunk(ci):
        rows = chunk_rows(ci)
        tables = [[ref[j] for j in range(N_PAIR)] for ref in table_refs]
        y, new_states = _wkv_apply(tables, v_s[rows, :], etot_s[...],
                                   [st_s[j] for j in range(N_PAIR)])
        y_s[rows, :] = y
        for j in range(N_PAIR):
            st_s[j] = new_states[j]

    store_tables(*make_tables(0))

    def chunk_body(ci, carry):
        new_tables = make_tables(ci)
        apply_chunk(ci - 1)
        store_tables(*new_tables)
        return carry

    lax.fori_loop(1, n_chunk, chunk_body, 0)
    apply_chunk(n_chunk - 1)

    y = y_s[...]
    mean = head_sum(y) * (1.0 / HEAD)
    d = y - mean
    var = head_sum(d * d) * (1.0 / HEAD)
    yn = d * lax.rsqrt(var + GN_EPS) * lnw_ref[...] + lnb_ref[...] + bon_s[...]
    yb = _dot(yn * g_s[...], woutb_ref[...])
    gb = jnp.dot(h, win_ref[:, OFF_GB:OFF_GB + D_MODEL], preferred_element_type=F32)
    merged = m_s[...] + jax.nn.sigmoid(gb) * yb
    o_ref[...] = x + _dot(merged, wo_ref[...])


def _mixer(x, n_batch, gain, win, convw, wouta, mu, w0, wdec, a0, wicl, wgate, kk, ka, rk,
           lnw, lnb, woutb, wo):
    m = x.shape[0]
    tiles = m // n_batch // TM_MIX
    head_of = jnp.arange(D_RWKV, dtype=jnp.int32) // HEAD
    bd = (head_of[:, None] == head_of[None, :]).astype(BF)
    idx = jnp.arange(CHUNK, dtype=jnp.int32)
    tril = (idx[None, :] <= idx[:, None]).astype(BF)
    consts = [gain, win, convw, wouta, mu, w0, wdec, a0, wicl, wgate, kk, ka, rk, lnw, lnb,
              woutb, wo, bd, tril]
    tok_spec = pl.BlockSpec((TM_MIX, D_MODEL), lambda bi, ti: (bi * tiles + ti, 0))
    vec = lambda n: pltpu.VMEM((TM_MIX, n), F32)
    return pl.pallas_call(
        _mixer_body,
        grid=(n_batch, tiles),
        in_specs=[tok_spec] + [_const_spec(c.shape) for c in consts],
        out_specs=tok_spec,
        out_shape=jax.ShapeDtypeStruct((m, D_MODEL), F32),
        scratch_shapes=[
            pltpu.VMEM((TM_MIX + HALO, D_CONV), F32),
            pltpu.VMEM((TM_MIX + HALO, COLS_B), F32),
            vec(D_RWKV), vec(D_RWKV), vec(D_RWKV), vec(D_RWKV), vec(D_RWKV), vec(D_RWKV),
            vec(D_RWKV), vec(D_RWKV), vec(D_RWKV), vec(D_MODEL),
            pltpu.VMEM((N_PAIR, PAIR, PAIR), F32),
            pltpu.VMEM((N_PAIR, 4 * CHUNK, PAIR), BF),
            pltpu.VMEM((N_PAIR, PAIR, PAIR), BF),
            pltpu.VMEM((N_PAIR, PAIR, PAIR), BF),
            pltpu.VMEM((N_PAIR, PAIR, PAIR), F32),
            pltpu.VMEM((N_PAIR, PAIR, PAIR), BF),
            pltpu.VMEM((N_PAIR, PAIR, PAIR), F32),
            pltpu.VMEM((1, D_RWKV), F32),
        ],
        compiler_params=pltpu.CompilerParams(
            dimension_semantics=("arbitrary", "arbitrary"), vmem_limit_bytes=VMEM_LIMIT),
        name="mixer",
    )(x, *consts)


def kernel(x, ffn1_norm, ffn1_w_gate, ffn1_w_up, ffn1_w_down, mix_norm, w_in, conv_w, w_out_a, mu_b, w0, w_decay_up, a0, w_iclr_up, w_gate_up, k_k, k_a, r_k, ln_x_w, ln_x_b, w_out_b, w_o, ffn2_norm, ffn2_w_gate, ffn2_w_up, ffn2_w_down, final_norm):
    n_batch, seq, d = x.shape
    assert d == D_MODEL and seq % TM_MIX == 0 and (n_batch * seq) % TM_FFN == 0
    assert ffn1_norm.shape[0] == 1, "single layer"
    row = lambda t: t.reshape(1, -1).astype(F32)
    bf = lambda t: t.astype(BF)
    xf = x.reshape(n_batch * seq, d)

    x1 = _ffn(xf, row(ffn1_norm[0]), bf(ffn1_w_gate[0]), bf(ffn1_w_up[0]), bf(ffn1_w_down[0]))

    zeros_lora = jnp.zeros((64, D_RWKV), F32)
    wdec = jnp.concatenate([w_decay_up[0], zeros_lora], axis=0)
    wicl = jnp.concatenate([zeros_lora, w_iclr_up[0]], axis=0)
    x2 = _mixer(x1, n_batch, row(mix_norm[0]), bf(w_in[0]), conv_w[0].astype(F32),
                bf(w_out_a[0]), row(mu_b[0]), row(w0[0]), wdec, row(a0[0]), wicl,
                bf(w_gate_up[0]), row(k_k[0]), row(k_a[0]), row(r_k[0]), row(ln_x_w[0]),
                row(ln_x_b[0]), bf(w_out_b[0]), bf(w_o[0]))

    out = _ffn(x2, row(ffn2_norm[0]), bf(ffn2_w_gate[0]), bf(ffn2_w_up[0]), bf(ffn2_w_down[0]),
               final_gain=row(final_norm))
    return out.reshape(n_batch, seq, d)
```

```python
import functools

import jax
import jax.numpy as jnp
from jax import lax
from jax.experimental import pallas as pl
from jax.experimental.pallas import tpu as pltpu

F32 = jnp.float32
BF = jnp.bfloat16

D_MODEL = 1024
D_CONV = 512
D_RWKV = 512
HEAD = 64
D_FF = 2816
COLS_A = 3 * D_CONV
COLS_B = 3 * D_RWKV + 64 + 64 + 128
OFF_B = COLS_A
OFF_GA = COLS_A + COLS_B
OFF_GB = OFF_GA + D_MODEL
RMS_EPS = 1e-6
GN_EPS = 64e-5

CHUNK = 64
PAIR = 2 * HEAD
N_PAIR = D_RWKV // PAIR
HALO = 8
TM_FFN = 512
TF_FFN = 256
TM_MIX = 512
CHUNK_GROUP = 4
VMEM_LIMIT = 56 * 1024 * 1024


def _dot(a, b):
    return jnp.dot(a.astype(BF), b.astype(BF), preferred_element_type=F32)


def _dot_nt(a, b):
    return lax.dot_general(a.astype(BF), b.astype(BF), (((1,), (1,)), ((), ())),
                           preferred_element_type=F32)


def _split2(x):
    hi = x.astype(BF)
    lo = (x - hi.astype(F32)).astype(BF)
    return hi, lo


def _dot3(a, b):
    ah, al = _split2(a)
    bh, bl = _split2(b)
    return (jnp.dot(ah, bh, preferred_element_type=F32)
            + (jnp.dot(ah, bl, preferred_element_type=F32)
               + jnp.dot(al, bh, preferred_element_type=F32)))


def _dot_exact_rhs(a, b_bf):
    ah, al = _split2(a)
    return (jnp.dot(ah, b_bf, preferred_element_type=F32)
            + jnp.dot(al, b_bf, preferred_element_type=F32))


def _rms_norm(x, gain):
    return x * lax.rsqrt(jnp.mean(x * x, axis=-1, keepdims=True) + RMS_EPS) * gain


def _ffn_body(*refs, final_norm):
    if final_norm:
        x_ref, gain_ref, wg_ref, wu_ref, wd_ref, fn_ref, o_ref, act_ref = refs
    else:
        x_ref, gain_ref, wg_ref, wu_ref, wd_ref, o_ref, act_ref = refs
    x = x_ref[...]
    h = _rms_norm(x, gain_ref[...]).astype(BF)
    for c in range(D_FF // TF_FFN):
        sl = slice(c * TF_FFN, (c + 1) * TF_FFN)
        g = jnp.dot(h, wg_ref[:, sl], preferred_element_type=F32)
        u = jnp.dot(h, wu_ref[:, sl], preferred_element_type=F32)
        act_ref[:, sl] = (g * jax.nn.sigmoid(g) * u).astype(BF)
    y = x + 0.5 * jnp.dot(act_ref[...], wd_ref[...], preferred_element_type=F32)
    if final_norm:
        y = _rms_norm(y, fn_ref[...])
    o_ref[...] = y


def _const_spec(shape):
    return pl.BlockSpec(shape, lambda *_: (0,) * len(shape), pipeline_mode=pl.Buffered(1))


def _ffn(x, gain, wg, wu, wd, final_gain=None):
    m = x.shape[0]
    final_norm = final_gain is not None
    in_specs = [
        pl.BlockSpec((TM_FFN, D_MODEL), lambda i: (i, 0)),
        _const_spec((1, D_MODEL)),
        _const_spec((D_MODEL, D_FF)),
        _const_spec((D_MODEL, D_FF)),
        _const_spec((D_FF, D_MODEL)),
    ]
    args = [x, gain, wg, wu, wd]
    if final_norm:
        in_specs.append(_const_spec((1, D_MODEL)))
        args.append(final_gain)
    return pl.pallas_call(
        functools.partial(_ffn_body, final_norm=final_norm),
        grid=(m // TM_FFN,),
        in_specs=in_specs,
        out_specs=pl.BlockSpec((TM_FFN, D_MODEL), lambda i: (i, 0)),
        out_shape=jax.ShapeDtypeStruct((m, D_MODEL), F32),
        scratch_shapes=[pltpu.VMEM((TM_FFN, D_FF), BF)],
        compiler_params=pltpu.CompilerParams(
            dimension_semantics=("arbitrary",), vmem_limit_bytes=VMEM_LIMIT),
        name="ffn_final" if final_norm else "ffn",
    )(*args)


def _pair_index():
    row = lax.broadcasted_iota(jnp.int32, (PAIR, PAIR), 0)
    lane = lax.broadcasted_iota(jnp.int32, (PAIR, PAIR), 1)
    return row, lane


def _interleave(*gens):
    live = list(gens)
    while live:
        for g in list(live):
            try:
                next(g)
            except StopIteration:
                live.remove(g)


def _tri_inverse_stages(l_bds, out):
    row, lane = _pair_index()
    zero = jnp.zeros((PAIR, PAIR), F32)
    eye = jnp.where(row == lane, 1.0, 0.0).astype(F32)

    def off(m):
        return ((row // (2 * m)) == (lane // (2 * m))) & ((row % (2 * m)) >= m) & ((lane % (2 * m)) < m)

    ts = [eye + jnp.where(off(1), l, zero) for l in l_bds]
    m = 2
    while m < CHUNK:
        mask = off(m)
        tl = [_dot(t, jnp.where(mask, l, zero)) for t, l in zip(ts, l_bds)]
        yield
        ts = [t + _dot(x, t) for t, x in zip(ts, tl)]
        yield
        m *= 2
    out.extend(ts)


def _wkv_tables_stages(chunks, tril, out):
    c = CHUNK
    head0 = lax.broadcasted_iota(jnp.int32, (c, PAIR), 1) < HEAD
    row, lane = _pair_index()
    t_idx = row % c
    s_idx = lane % c
    blockdiag = (row // c) == (lane // c)
    zero = jnp.zeros((c, PAIR), F32)
    zero2 = jnp.zeros((PAIR, PAIR), F32)

    e_tots, lhs0, lhs1, rhs0, rhs1, arstk, bkh, v_swap = [], [], [], [], [], [], [], []
    for r, k, v, a, b, lw in chunks:
        h1 = lw.astype(BF)
        r1 = lw - h1.astype(F32)
        h2 = r1.astype(BF)
        h3 = (r1 - h2.astype(F32)).astype(BF)
        cs = (jnp.dot(tril, h1, preferred_element_type=F32)
              + (jnp.dot(tril, h2, preferred_element_type=F32)
                 + jnp.dot(tril, h3, preferred_element_type=F32)))
        tot = cs[c - 1:c, :]
        e_tots.append(jnp.exp(tot))
        at = a * jnp.exp(cs - lw)
        rt = r * jnp.exp(cs)
        e_n = jnp.exp(-cs)
        bt = b * e_n
        kt = k * e_n
        e_t = jnp.exp(tot - cs)
        bh = b * e_t
        kh = k * e_t
        for j in range(N_PAIR):
            sl = slice(j * PAIR, (j + 1) * PAIR)
            ar0 = jnp.concatenate([jnp.where(head0, at[:, sl], zero), jnp.where(head0, rt[:, sl], zero)], axis=0)
            ar1 = jnp.concatenate([jnp.where(head0, zero, at[:, sl]), jnp.where(head0, zero, rt[:, sl])], axis=0)
            lhs0.append(ar0)
            lhs1.append(ar1)
            rhs0.append(jnp.concatenate([bt[:, sl], kt[:, sl]], axis=0))
            rhs1.append(jnp.concatenate([kt[:, sl], bt[:, sl]], axis=0))
            arstk.append(jnp.concatenate([ar0, ar1], axis=0))
            bkh.append(jnp.concatenate([bh[:, sl], kh[:, sl]], axis=0))
            v_swap.append(jnp.concatenate([jnp.where(head0, zero, v[:, sl]),
                                           jnp.where(head0, v[:, sl], zero)], axis=0))
    yield
    g0 = [_dot_nt(x, y) for x, y in zip(lhs0, rhs0)]
    g1 = [_dot_nt(x, y) for x, y in zip(lhs1, rhs1)]
    yield
    aa = [jnp.where(s_idx < t_idx, jnp.concatenate([x[:c], y[:c]], axis=0), zero2) for x, y in zip(g0, g1)]
    arr = [jnp.where(s_idx <= t_idx, jnp.concatenate([x[c:], y[c:]], axis=0), zero2) for x, y in zip(g0, g1)]
    l_bd = [jnp.where(blockdiag, x, zero2) for x in aa]
    ak_ad = [jnp.where(blockdiag, zero2, x) for x in aa]
    t_inv = []
    yield from _tri_inverse_stages(l_bd, t_inv)
    akv = [_dot(x, y) for x, y in zip(ak_ad, v_swap)]
    yield
    for i in range(len(chunks)):
        sl = slice(i * N_PAIR, (i + 1) * N_PAIR)
        out.append(((arstk[sl], t_inv[sl], arr[sl], akv[sl], bkh[sl], v_swap[sl]), e_tots[i]))


def _wkv_apply_stages(tables, v, e_tot, states, out):
    c = CHUNK
    arstk, t_inv, arr, akv, bkh, v_swap = tables
    head0 = lax.broadcasted_iota(jnp.int32, (c, PAIR), 1) < HEAD
    row, lane = _pair_index()
    blockdiag = (row // c) == (lane // c)
    zero2 = jnp.zeros((PAIR, PAIR), F32)
    xs = [_dot_nt(x, s) for x, s in zip(arstk, states)]
    yield
    x_bd = [jnp.concatenate([z[0:c], z[2 * c:3 * c]], axis=0) for z in xs]
    rs_bd = [jnp.concatenate([z[c:2 * c], z[3 * c:4 * c]], axis=0) for z in xs]
    u_bd = [_dot(t, x + w) for t, x, w in zip(t_inv, x_bd, akv)]
    yield
    y_bd = [_dot(m, u + w) + z for m, u, w, z in zip(arr, u_bd, v_swap, rs_bd)]
    ys = [jnp.where(head0, z[:c], z[c:]) for z in y_bd]
    new_states = []
    for j in range(N_PAIR):
        sl = slice(j * PAIR, (j + 1) * PAIR)
        uv = jnp.concatenate([u_bd[j][:c] + u_bd[j][c:], v[:, sl]], axis=0)
        upd = _dot(uv.T, bkh[j])
        new_states.append(jnp.where(blockdiag, states[j] * e_tot[:, sl] + upd, zero2))
    yield
    out.append((jnp.concatenate(ys, axis=1), new_states))


def _mixer_body(x_ref, gain_ref, win_ref, convw_ref, wouta_ref, mu_ref, w0_ref, wdec_ref,
                a0_ref, wicl_ref, wgate_ref, kk_ref, ka_ref, rk_ref, lnw_ref, lnb_ref,
                woutb_ref, wo_ref, bd_ref, tril_ref,
                o_ref,
                cu_buf, pb_buf, r_s, k_s, v_s, a_s, b_s, lw_s, g_s, bon_s, y_s, m_s, st_s):
    tm = TM_MIX

    @pl.when(pl.program_id(1) == 0)
    def _():
        cu_buf[0:HALO, :] = jnp.zeros((HALO, D_CONV), F32)
        pb_buf[0:HALO, :] = jnp.zeros((HALO, COLS_B), F32)
        st_s[...] = jnp.zeros_like(st_s)

    x = x_ref[...]
    h = _rms_norm(x, gain_ref[...]).astype(BF)
    bd = bd_ref[...]

    def head_sum(z):
        return _dot_exact_rhs(z, bd)

    pa = jnp.dot(h, win_ref[:, 0:COLS_A], preferred_element_type=F32)
    cu = pa[:, D_CONV:2 * D_CONV] * pa[:, 2 * D_CONV:3 * D_CONV]
    cu_buf[HALO:HALO + tm, :] = cu
    cw = convw_ref[...]
    conv = (cw[2:3] * cu + cw[1:2] * cu_buf[pl.ds(HALO - 1, tm), :]
            + cw[0:1] * cu_buf[pl.ds(HALO - 2, tm), :])
    cu_buf[0:HALO, :] = cu_buf[tm:tm + HALO, :]
    ya = _dot(pa[:, 0:D_CONV] * conv, wouta_ref[...])
    ga = jnp.dot(h, win_ref[:, OFF_GA:OFF_GB], preferred_element_type=F32)
    m_s[...] = jax.nn.sigmoid(ga) * ya

    pb = jnp.dot(h, win_ref[:, OFF_B:OFF_GA], preferred_element_type=F32)
    pb_buf[HALO:HALO + tm, :] = pb
    prev = pb_buf[pl.ds(HALO - 1, tm), :]
    pb_buf[0:HALO, :] = pb_buf[tm:tm + HALO, :]
    pbm = pb + (prev - pb) * mu_ref[...]
    r = pbm[:, 0:D_RWKV]
    k = pbm[:, D_RWKV:2 * D_RWKV]
    v = pbm[:, 2 * D_RWKV:3 * D_RWKV]
    xwa = pbm[:, 3 * D_RWKV:3 * D_RWKV + 128]
    xg = pbm[:, 3 * D_RWKV + 128:COLS_B]

    z = w0_ref[...] + _dot3(jnp.tanh(xwa), wdec_ref[...])
    nz = -z
    softplus = jnp.maximum(nz, 0.0) + jnp.log1p(jnp.exp(-jnp.abs(nz)))
    w_log = -softplus - 0.5
    lw_s[...] = -jnp.exp(w_log)
    iclr = jax.nn.sigmoid(a0_ref[...] + _dot3(xwa, wicl_ref[...]))
    g_s[...] = _dot(jax.nn.sigmoid(xg), wgate_ref[...])

    kk = k * kk_ref[...]
    kk = kk / jnp.maximum(jnp.sqrt(head_sum(kk * kk)), 1e-12)
    k2 = k * (1.0 + (iclr - 1.0) * ka_ref[...])
    r_s[...] = r
    k_s[...] = k2
    v_s[...] = v
    a_s[...] = -kk
    b_s[...] = kk * iclr
    bon_s[...] = head_sum(r * k2 * rk_ref[...]) * v

    tril = tril_ref[...]
    n_group = tm // CHUNK // CHUNK_GROUP

    def chunk_rows(ci):
        return slice(ci * CHUNK, (ci + 1) * CHUNK)

    group_tables = [[] for _ in range(n_group)]
    states = [[st_s[j] for j in range(N_PAIR)]]

    def tables_stages(gi):
        chunks = []
        for ci in range(gi * CHUNK_GROUP, (gi + 1) * CHUNK_GROUP):
            rows = chunk_rows(ci)
            chunks.append((r_s[rows, :], k_s[rows, :], v_s[rows, :], a_s[rows, :], b_s[rows, :],
                           lw_s[rows, :]))
        yield from _wkv_tables_stages(chunks, tril, group_tables[gi])

    def apply_stages(gi):
        for i in range(CHUNK_GROUP):
            rows = chunk_rows(gi * CHUNK_GROUP + i)
            tables, e_tot = group_tables[gi][i]
            res = []
            yield from _wkv_apply_stages(tables, v_s[rows, :], e_tot, states[0], res)
            y_s[rows, :] = res[0][0]
            states[0] = res[0][1]

    _interleave(tables_stages(0))
    for gi in range(n_group):
        if gi + 1 < n_group:
            _interleave(tables_stages(gi + 1), apply_stages(gi))
        else:
            _interleave(apply_stages(gi))
    for j in range(N_PAIR):
        st_s[j] = states[0][j]

    y = y_s[...]
    mean = head_sum(y) * (1.0 / HEAD)
    d = y - mean
    var = head_sum(d * d) * (1.0 / HEAD)
    yn = d * lax.rsqrt(var + GN_EPS) * lnw_ref[...] + lnb_ref[...] + bon_s[...]
    yb = _dot(yn * g_s[...], woutb_ref[...])
    gb = jnp.dot(h, win_ref[:, OFF_GB:OFF_GB + D_MODEL], preferred_element_type=F32)
    merged = m_s[...] + jax.nn.sigmoid(gb) * yb
    o_ref[...] = x + _dot(merged, wo_ref[...])


def _mixer(x, n_batch, gain, win, convw, wouta, mu, w0, wdec, a0, wicl, wgate, kk, ka, rk,
           lnw, lnb, woutb, wo):
    m = x.shape[0]
    tiles = m // n_batch // TM_MIX
    head_of = jnp.arange(D_RWKV, dtype=jnp.int32) // HEAD
    bd = (head_of[:, None] == head_of[None, :]).astype(BF)
    idx = jnp.arange(CHUNK, dtype=jnp.int32)
    tril = (idx[None, :] <= idx[:, None]).astype(BF)
    consts = [gain, win, convw, wouta, mu, w0, wdec, a0, wicl, wgate, kk, ka, rk, lnw, lnb,
              woutb, wo, bd, tril]
    tok_spec = pl.BlockSpec((TM_MIX, D_MODEL), lambda bi, ti: (bi * tiles + ti, 0))
    vec = lambda n: pltpu.VMEM((TM_MIX, n), F32)
    return pl.pallas_call(
        _mixer_body,
        grid=(n_batch, tiles),
        in_specs=[tok_spec] + [_const_spec(c.shape) for c in consts],
        out_specs=tok_spec,
        out_shape=jax.ShapeDtypeStruct((m, D_MODEL), F32),
        scratch_shapes=[
            pltpu.VMEM((TM_MIX + HALO, D_CONV), F32),
            pltpu.VMEM((TM_MIX + HALO, COLS_B), F32),
            vec(D_RWKV), vec(D_RWKV), vec(D_RWKV), vec(D_RWKV), vec(D_RWKV), vec(D_RWKV),
            vec(D_RWKV), vec(D_RWKV), vec(D_RWKV), vec(D_MODEL),
            pltpu.VMEM((N_PAIR, PAIR, PAIR), F32),
        ],
        compiler_params=pltpu.CompilerParams(
            dimension_semantics=("arbitrary", "arbitrary"), vmem_limit_bytes=VMEM_LIMIT),
        name="mixer",
    )(x, *consts)


def kernel(x, ffn1_norm, ffn1_w_gate, ffn1_w_up, ffn1_w_down, mix_norm, w_in, conv_w, w_out_a, mu_b, w0, w_decay_up, a0, w_iclr_up, w_gate_up, k_k, k_a, r_k, ln_x_w, ln_x_b, w_out_b, w_o, ffn2_norm, ffn2_w_gate, ffn2_w_up, ffn2_w_down, final_norm):
    n_batch, seq, d = x.shape
    assert d == D_MODEL and seq % TM_MIX == 0 and (n_batch * seq) % TM_FFN == 0
    assert ffn1_norm.shape[0] == 1, "single layer"
    row = lambda t: t.reshape(1, -1).astype(F32)
    bf = lambda t: t.astype(BF)
    xf = x.reshape(n_batch * seq, d)

    x1 = _ffn(xf, row(ffn1_norm[0]), bf(ffn1_w_gate[0]), bf(ffn1_w_up[0]), bf(ffn1_w_down[0]))

    zeros_lora = jnp.zeros((64, D_RWKV), F32)
    wdec = jnp.concatenate([w_decay_up[0], zeros_lora], axis=0)
    wicl = jnp.concatenate([zeros_lora, w_iclr_up[0]], axis=0)
    x2 = _mixer(x1, n_batch, row(mix_norm[0]), bf(w_in[0]), conv_w[0].astype(F32),
                bf(w_out_a[0]), row(mu_b[0]), row(w0[0]), wdec, row(a0[0]), wicl,
                bf(w_gate_up[0]), row(k_k[0]), row(k_a[0]), row(r_k[0]), row(ln_x_w[0]),
                row(ln_x_b[0]), bf(w_out_b[0]), bf(w_o[0]))

    out = _ffn(x2, row(ffn2_norm[0]), bf(ffn2_w_gate[0]), bf(ffn2_w_up[0]), bf(ffn2_w_down[0]),
               final_gain=row(final_norm))
    return out.reshape(n_batch, seq, d)
```

```python
import functools

import jax
import jax.numpy as jnp
from jax import lax
from jax.experimental import pallas as pl
from jax.experimental.pallas import tpu as pltpu

F32 = jnp.float32
BF = jnp.bfloat16

D_MODEL = 1024
D_CONV = 512
D_RWKV = 512
HEAD = 64
D_FF = 2816
COLS_A = 3 * D_CONV
COLS_B = 3 * D_RWKV + 64 + 64 + 128
OFF_B = COLS_A
OFF_GA = COLS_A + COLS_B
OFF_GB = OFF_GA + D_MODEL
RMS_EPS = 1e-6
GN_EPS = 64e-5

CHUNK = 64
PAIR = 2 * HEAD
N_PAIR = D_RWKV // PAIR
HALO = 8
TM_FFN = 512
TF_FFN = 256
TM_MIX = 512
CHUNK_GROUP = 4
VMEM_LIMIT = 56 * 1024 * 1024


def _dot(a, b):
    return jnp.dot(a.astype(BF), b.astype(BF), preferred_element_type=F32)


def _dot_nt(a, b):
    return lax.dot_general(a.astype(BF), b.astype(BF), (((1,), (1,)), ((), ())),
                           preferred_element_type=F32)


def _split2(x):
    hi = x.astype(BF)
    lo = (x - hi.astype(F32)).astype(BF)
    return hi, lo


def _dot3(a, b):
    ah, al = _split2(a)
    bh, bl = _split2(b)
    return (jnp.dot(ah, bh, preferred_element_type=F32)
            + (jnp.dot(ah, bl, preferred_element_type=F32)
               + jnp.dot(al, bh, preferred_element_type=F32)))


def _dot_exact_rhs(a, b_bf):
    ah, al = _split2(a)
    return (jnp.dot(ah, b_bf, preferred_element_type=F32)
            + jnp.dot(al, b_bf, preferred_element_type=F32))


def _rms_norm(x, gain):
    return x * lax.rsqrt(jnp.mean(x * x, axis=-1, keepdims=True) + RMS_EPS) * gain


def _ffn_body(*refs, final_norm):
    if final_norm:
        x_ref, gain_ref, wg_ref, wu_ref, wd_ref, fn_ref, o_ref, act_ref = refs
    else:
        x_ref, gain_ref, wg_ref, wu_ref, wd_ref, o_ref, act_ref = refs
    x = x_ref[...]
    h = _rms_norm(x, gain_ref[...])
    for c in range(D_FF // TF_FFN):
        sl = slice(c * TF_FFN, (c + 1) * TF_FFN)
        g = jnp.dot(h, wg_ref[:, sl], preferred_element_type=F32)
        u = jnp.dot(h, wu_ref[:, sl], preferred_element_type=F32)
        act_ref[:, sl] = g * jax.nn.sigmoid(g) * u
    y = x + 0.5 * jnp.dot(act_ref[...], wd_ref[...], preferred_element_type=F32)
    if final_norm:
        y = _rms_norm(y, fn_ref[...])
    o_ref[...] = y


def _const_spec(shape):
    return pl.BlockSpec(shape, lambda *_: (0,) * len(shape), pipeline_mode=pl.Buffered(1))


def _ffn(x, gain, wg, wu, wd, final_gain=None):
    m = x.shape[0]
    final_norm = final_gain is not None
    in_specs = [
        pl.BlockSpec((TM_FFN, D_MODEL), lambda i: (i, 0)),
        _const_spec((1, D_MODEL)),
        _const_spec((D_MODEL, D_FF)),
        _const_spec((D_MODEL, D_FF)),
        _const_spec((D_FF, D_MODEL)),
    ]
    args = [x, gain, wg, wu, wd]
    if final_norm:
        in_specs.append(_const_spec((1, D_MODEL)))
        args.append(final_gain)
    return pl.pallas_call(
        functools.partial(_ffn_body, final_norm=final_norm),
        grid=(m // TM_FFN,),
        in_specs=in_specs,
        out_specs=pl.BlockSpec((TM_FFN, D_MODEL), lambda i: (i, 0)),
        out_shape=jax.ShapeDtypeStruct((m, D_MODEL), F32),
        scratch_shapes=[pltpu.VMEM((TM_FFN, D_FF), F32)],
        compiler_params=pltpu.CompilerParams(
            dimension_semantics=("arbitrary",), vmem_limit_bytes=VMEM_LIMIT),
        name="ffn_final" if final_norm else "ffn",
    )(*args)


def _pair_index():
    row = lax.broadcasted_iota(jnp.int32, (PAIR, PAIR), 0)
    lane = lax.broadcasted_iota(jnp.int32, (PAIR, PAIR), 1)
    return row, lane


def _interleave(*gens):
    live = list(gens)
    while live:
        for g in list(live):
            try:
                next(g)
            except StopIteration:
                live.remove(g)


def _tri_inverse_stages(l_bds, out):
    row, lane = _pair_index()
    zero = jnp.zeros((PAIR, PAIR), F32)
    eye = jnp.where(row == lane, 1.0, 0.0).astype(F32)

    def off(m):
        return ((row // (2 * m)) == (lane // (2 * m))) & ((row % (2 * m)) >= m) & ((lane % (2 * m)) < m)

    ts = [eye + jnp.where(off(1), l, zero) for l in l_bds]
    m = 2
    while m < CHUNK:
        mask = off(m)
        tl = [_dot(t, jnp.where(mask, l, zero)) for t, l in zip(ts, l_bds)]
        yield
        ts = [t + _dot(x, t) for t, x in zip(ts, tl)]
        yield
        m *= 2
    out.extend(ts)


def _wkv_tables_stages(chunks, tril, out):
    c = CHUNK
    head0 = lax.broadcasted_iota(jnp.int32, (c, PAIR), 1) < HEAD
    row, lane = _pair_index()
    t_idx = row % c
    s_idx = lane % c
    blockdiag = (row // c) == (lane // c)
    zero = jnp.zeros((c, PAIR), F32)
    zero2 = jnp.zeros((PAIR, PAIR), F32)

    e_tots, lhs0, lhs1, rhs0, rhs1, arstk, bkh, v_swap = [], [], [], [], [], [], [], []
    for r, k, v, a, b, lw in chunks:
        h1 = lw.astype(BF)
        r1 = lw - h1.astype(F32)
        h2 = r1.astype(BF)
        h3 = (r1 - h2.astype(F32)).astype(BF)
        cs = (jnp.dot(tril, h1, preferred_element_type=F32)
              + (jnp.dot(tril, h2, preferred_element_type=F32)
                 + jnp.dot(tril, h3, preferred_element_type=F32)))
        tot = cs[c - 1:c, :]
        e_tots.append(jnp.exp(tot))
        at = a * jnp.exp(cs - lw)
        rt = r * jnp.exp(cs)
        e_n = jnp.exp(-cs)
        bt = b * e_n
        kt = k * e_n
        e_t = jnp.exp(tot - cs)
        bh = b * e_t
        kh = k * e_t
        for j in range(N_PAIR):
            sl = slice(j * PAIR, (j + 1) * PAIR)
            ar0 = jnp.concatenate([jnp.where(head0, at[:, sl], zero), jnp.where(head0, rt[:, sl], zero)], axis=0)
            ar1 = jnp.concatenate([jnp.where(head0, zero, at[:, sl]), jnp.where(head0, zero, rt[:, sl])], axis=0)
            lhs0.append(ar0)
            lhs1.append(ar1)
            rhs0.append(jnp.concatenate([bt[:, sl], kt[:, sl]], axis=0))
            rhs1.append(jnp.concatenate([kt[:, sl], bt[:, sl]], axis=0))
            arstk.append(jnp.concatenate([ar0, ar1], axis=0))
            bkh.append(jnp.concatenate([bh[:, sl], kh[:, sl]], axis=0))
            v_swap.append(jnp.concatenate([jnp.where(head0, zero, v[:, sl]),
                                           jnp.where(head0, v[:, sl], zero)], axis=0))
    yield
    g0 = [_dot_nt(x, y) for x, y in zip(lhs0, rhs0)]
    g1 = [_dot_nt(x, y) for x, y in zip(lhs1, rhs1)]
    yield
    aa = [jnp.where(s_idx < t_idx, jnp.concatenate([x[:c], y[:c]], axis=0), zero2) for x, y in zip(g0, g1)]
    arr = [jnp.where(s_idx <= t_idx, jnp.concatenate([x[c:], y[c:]], axis=0), zero2) for x, y in zip(g0, g1)]
    l_bd = [jnp.where(blockdiag, x, zero2) for x in aa]
    ak_ad = [jnp.where(blockdiag, zero2, x) for x in aa]
    t_inv = []
    yield from _tri_inverse_stages(l_bd, t_inv)
    akv = [_dot(x, y) for x, y in zip(ak_ad, v_swap)]
    yield
    for i in range(len(chunks)):
        sl = slice(i * N_PAIR, (i + 1) * N_PAIR)
        out.append(((arstk[sl], t_inv[sl], arr[sl], akv[sl], bkh[sl], v_swap[sl]), e_tots[i]))


def _wkv_apply_stages(tables, v, e_tot, states, out):
    c = CHUNK
    arstk, t_inv, arr, akv, bkh, v_swap = tables
    head0 = lax.broadcasted_iota(jnp.int32, (c, PAIR), 1) < HEAD
    row, lane = _pair_index()
    blockdiag = (row // c) == (lane // c)
    zero2 = jnp.zeros((PAIR, PAIR), F32)
    xs = [_dot_nt(x, s) for x, s in zip(arstk, states)]
    yield
    x_bd = [jnp.concatenate([z[0:c], z[2 * c:3 * c]], axis=0) for z in xs]
    rs_bd = [jnp.concatenate([z[c:2 * c], z[3 * c:4 * c]], axis=0) for z in xs]
    u_bd = [_dot(t, x + w) for t, x, w in zip(t_inv, x_bd, akv)]
    yield
    y_bd = [_dot(m, u + w) + z for m, u, w, z in zip(arr, u_bd, v_swap, rs_bd)]
    ys = [jnp.where(head0, z[:c], z[c:]) for z in y_bd]
    new_states = []
    for j in range(N_PAIR):
        sl = slice(j * PAIR, (j + 1) * PAIR)
        uv = jnp.concatenate([u_bd[j][:c] + u_bd[j][c:], v[:, sl]], axis=0)
        upd = _dot(uv.T, bkh[j])
        new_states.append(jnp.where(blockdiag, states[j] * e_tot[:, sl] + upd, zero2))
    yield
    out.append((jnp.concatenate(ys, axis=1), new_states))


def _mixer_body(x_ref, gain_ref, win_ref, convw_ref, wouta_ref, mu_ref, w0_ref, wdec_ref,
                a0_ref, wicl_ref, wgate_ref, kk_ref, ka_ref, rk_ref, lnw_ref, lnb_ref,
                woutb_ref, wo_ref, bd_ref, tril_ref,
                o_ref,
                cu_buf, pb_buf, r_s, k_s, v_s, a_s, b_s, lw_s, g_s, bon_s, y_s, m_s, st_s):
    tm = TM_MIX

    @pl.when(pl.program_id(1) == 0)
    def _():
        cu_buf[0:HALO, :] = jnp.zeros((HALO, D_CONV), F32)
        pb_buf[0:HALO, :] = jnp.zeros((HALO, COLS_B), F32)
        st_s[...] = jnp.zeros_like(st_s)

    x = x_ref[...]
    h = _rms_norm(x, gain_ref[...]).astype(BF)
    bd = bd_ref[...]

    def head_sum(z):
        half = D_RWKV // 2
        s = _dot_exact_rhs(jnp.concatenate([z[:, :half], z[:, half:]], axis=0), bd)
        return jnp.concatenate([s[:tm], s[tm:]], axis=1)

    pa = jnp.dot(h, win_ref[:, 0:COLS_A], preferred_element_type=F32)
    cu = pa[:, D_CONV:2 * D_CONV] * pa[:, 2 * D_CONV:3 * D_CONV]
    cu_buf[HALO:HALO + tm, :] = cu
    cw = convw_ref[...]
    conv = (cw[2:3] * cu + cw[1:2] * cu_buf[pl.ds(HALO - 1, tm), :]
            + cw[0:1] * cu_buf[pl.ds(HALO - 2, tm), :])
    cu_buf[0:HALO, :] = cu_buf[tm:tm + HALO, :]
    ya = _dot(pa[:, 0:D_CONV] * conv, wouta_ref[...])
    ga = jnp.dot(h, win_ref[:, OFF_GA:OFF_GB], preferred_element_type=F32)
    m_s[...] = jax.nn.sigmoid(ga) * ya

    pb = jnp.dot(h, win_ref[:, OFF_B:OFF_GA], preferred_element_type=F32)
    pb_buf[HALO:HALO + tm, :] = pb
    prev = pb_buf[pl.ds(HALO - 1, tm), :]
    pb_buf[0:HALO, :] = pb_buf[tm:tm + HALO, :]
    pbm = pb + (prev - pb) * mu_ref[...]
    r = pbm[:, 0:D_RWKV]
    k = pbm[:, D_RWKV:2 * D_RWKV]
    v = pbm[:, 2 * D_RWKV:3 * D_RWKV]
    xwa = pbm[:, 3 * D_RWKV:3 * D_RWKV + 128]
    xg = pbm[:, 3 * D_RWKV + 128:COLS_B]

    z = w0_ref[...] + _dot3(jnp.tanh(xwa), wdec_ref[...])
    nz = -z
    softplus = jnp.maximum(nz, 0.0) + jnp.log1p(jnp.exp(-jnp.abs(nz)))
    w_log = -softplus - 0.5
    lw_s[...] = -jnp.exp(w_log)
    iclr = jax.nn.sigmoid(a0_ref[...] + _dot3(xwa, wicl_ref[...]))
    g_s[...] = _dot(jax.nn.sigmoid(xg), wgate_ref[...])

    kk = k * kk_ref[...]
    kk = kk / jnp.maximum(jnp.sqrt(head_sum(kk * kk)), 1e-12)
    k2 = k * (1.0 + (iclr - 1.0) * ka_ref[...])
    r_s[...] = r
    k_s[...] = k2
    v_s[...] = v
    a_s[...] = -kk
    b_s[...] = kk * iclr
    bon_s[...] = head_sum(r * k2 * rk_ref[...]) * v

    tril = tril_ref[...]
    n_group = tm // CHUNK // CHUNK_GROUP

    def chunk_rows(ci):
        return slice(ci * CHUNK, (ci + 1) * CHUNK)

    group_tables = [[] for _ in range(n_group)]
    states = [[st_s[j] for j in range(N_PAIR)]]

    def tables_stages(gi):
        chunks = []
        for ci in range(gi * CHUNK_GROUP, (gi + 1) * CHUNK_GROUP):
            rows = chunk_rows(ci)
            chunks.append((r_s[rows, :], k_s[rows, :], v_s[rows, :], a_s[rows, :], b_s[rows, :],
                           lw_s[rows, :]))
        yield from _wkv_tables_stages(chunks, tril, group_tables[gi])

    def apply_stages(gi):
        for i in range(CHUNK_GROUP):
            rows = chunk_rows(gi * CHUNK_GROUP + i)
            tables, e_tot = group_tables[gi][i]
            res = []
            yield from _wkv_apply_stages(tables, v_s[rows, :], e_tot, states[0], res)
            y_s[rows, :] = res[0][0]
            states[0] = res[0][1]

    _interleave(tables_stages(0))
    for gi in range(n_group):
        if gi + 1 < n_group:
            _interleave(tables_stages(gi + 1), apply_stages(gi))
        else:
            _interleave(apply_stages(gi))
    for j in range(N_PAIR):
        st_s[j] = states[0][j]

    y = y_s[...]
    mean = head_sum(y) * (1.0 / HEAD)
    d = y - mean
    var = head_sum(d * d) * (1.0 / HEAD)
    yn = d * lax.rsqrt(var + GN_EPS) * lnw_ref[...] + lnb_ref[...] + bon_s[...]
    yb = _dot(yn * g_s[...], woutb_ref[...])
    gb = jnp.dot(h, win_ref[:, OFF_GB:OFF_GB + D_MODEL], preferred_element_type=F32)
    merged = m_s[...] + jax.nn.sigmoid(gb) * yb
    o_ref[...] = x + _dot(merged, wo_ref[...])


def _mixer(x, n_batch, gain, win, convw, wouta, mu, w0, wdec, a0, wicl, wgate, kk, ka, rk,
           lnw, lnb, woutb, wo):
    m = x.shape[0]
    tiles = m // n_batch // TM_MIX
    head_of = jnp.arange(D_RWKV // 2, dtype=jnp.int32) // HEAD
    bd = (head_of[:, None] == head_of[None, :]).astype(BF)
    idx = jnp.arange(CHUNK, dtype=jnp.int32)
    tril = (idx[None, :] <= idx[:, None]).astype(BF)
    consts = [gain, win, convw, wouta, mu, w0, wdec, a0, wicl, wgate, kk, ka, rk, lnw, lnb,
              woutb, wo, bd, tril]
    tok_spec = pl.BlockSpec((TM_MIX, D_MODEL), lambda bi, ti: (bi * tiles + ti, 0))
    vec = lambda n: pltpu.VMEM((TM_MIX, n), F32)
    return pl.pallas_call(
        _mixer_body,
        grid=(n_batch, tiles),
        in_specs=[tok_spec] + [_const_spec(c.shape) for c in consts],
        out_specs=tok_spec,
        out_shape=jax.ShapeDtypeStruct((m, D_MODEL), F32),
        scratch_shapes=[
            pltpu.VMEM((TM_MIX + HALO, D_CONV), F32),
            pltpu.VMEM((TM_MIX + HALO, COLS_B), F32),
            vec(D_RWKV), vec(D_RWKV), vec(D_RWKV), vec(D_RWKV), vec(D_RWKV), vec(D_RWKV),
            vec(D_RWKV), vec(D_RWKV), vec(D_RWKV), vec(D_MODEL),
            pltpu.VMEM((N_PAIR, PAIR, PAIR), F32),
        ],
        compiler_params=pltpu.CompilerParams(
            dimension_semantics=("arbitrary", "arbitrary"), vmem_limit_bytes=VMEM_LIMIT),
        name="mixer",
    )(x, *consts)


def kernel(x, ffn1_norm, ffn1_w_gate, ffn1_w_up, ffn1_w_down, mix_norm, w_in, conv_w, w_out_a, mu_b, w0, w_decay_up, a0, w_iclr_up, w_gate_up, k_k, k_a, r_k, ln_x_w, ln_x_b, w_out_b, w_o, ffn2_norm, ffn2_w_gate, ffn2_w_up, ffn2_w_down, final_norm):
    n_batch, seq, d = x.shape
    assert d == D_MODEL and seq % TM_MIX == 0 and (n_batch * seq) % TM_FFN == 0
    assert ffn1_norm.shape[0] == 1, "single layer"
    row = lambda t: t.reshape(1, -1).astype(F32)
    bf = lambda t: t.astype(BF)
    xf = x.reshape(n_batch * seq, d)

    x1 = _ffn(xf, row(ffn1_norm[0]), ffn1_w_gate[0], ffn1_w_up[0], ffn1_w_down[0])

    zeros_lora = jnp.zeros((64, D_RWKV), F32)
    wdec = jnp.concatenate([w_decay_up[0], zeros_lora], axis=0)
    wicl = jnp.concatenate([zeros_lora, w_iclr_up[0]], axis=0)
    x2 = _mixer(x1, n_batch, row(mix_norm[0]), bf(w_in[0]), conv_w[0].astype(F32),
                bf(w_out_a[0]), row(mu_b[0]), row(w0[0]), wdec, row(a0[0]), wicl,
                bf(w_gate_up[0]), row(k_k[0]), row(k_a[0]), row(r_k[0]), row(ln_x_w[0]),
                row(ln_x_b[0]), bf(w_out_b[0]), bf(w_o[0]))

    out = _ffn(x2, row(ffn2_norm[0]), ffn2_w_gate[0], ffn2_w_up[0], ffn2_w_down[0],
               final_gain=row(final_norm))
    return out.reshape(n_batch, seq, d)
```

```python
import functools

import jax
import jax.numpy as jnp
from jax import lax
from jax.experimental import pallas as pl
from jax.experimental.pallas import tpu as pltpu

F32 = jnp.float32
BF = jnp.bfloat16

D_MODEL = 1024
D_CONV = 512
D_RWKV = 512
HEAD = 64
D_FF = 2816
COLS_A = 3 * D_CONV
COLS_B = 3 * D_RWKV + 64 + 64 + 128
OFF_B = COLS_A
OFF_GA = COLS_A + COLS_B
OFF_GB = OFF_GA + D_MODEL
RMS_EPS = 1e-6
GN_EPS = 64e-5

CHUNK = 64
PAIR = 2 * HEAD
N_PAIR = D_RWKV // PAIR
HALO = 8
TM_FFN = 512
TF_FFN = 256
TM_MIX = 512
CHUNK_GROUP = 4
GATE_COLS = 256
VMEM_LIMIT = 56 * 1024 * 1024


def _dot(a, b):
    return jnp.dot(a.astype(BF), b.astype(BF), preferred_element_type=F32)


def _dot_nt(a, b):
    return lax.dot_general(a.astype(BF), b.astype(BF), (((1,), (1,)), ((), ())),
                           preferred_element_type=F32)


def _split2(x):
    hi = x.astype(BF)
    lo = (x - hi.astype(F32)).astype(BF)
    return hi, lo


def _dot3(a, b):
    ah, al = _split2(a)
    bh, bl = _split2(b)
    return (jnp.dot(ah, bh, preferred_element_type=F32)
            + (jnp.dot(ah, bl, preferred_element_type=F32)
               + jnp.dot(al, bh, preferred_element_type=F32)))


def _dot_exact_rhs(a, b_bf):
    ah, al = _split2(a)
    return (jnp.dot(ah, b_bf, preferred_element_type=F32)
            + jnp.dot(al, b_bf, preferred_element_type=F32))


def _rms_norm(x, gain):
    return x * lax.rsqrt(jnp.mean(x * x, axis=-1, keepdims=True) + RMS_EPS) * gain


def _ffn_body(*refs, final_norm):
    if final_norm:
        x_ref, gain_ref, wg_ref, wu_ref, wd_ref, fn_ref, o_ref, act_ref = refs
    else:
        x_ref, gain_ref, wg_ref, wu_ref, wd_ref, o_ref, act_ref = refs
    x = x_ref[...]
    h = _rms_norm(x, gain_ref[...])
    for c in range(D_FF // TF_FFN):
        sl = slice(c * TF_FFN, (c + 1) * TF_FFN)
        g = jnp.dot(h, wg_ref[:, sl], preferred_element_type=F32)
        u = jnp.dot(h, wu_ref[:, sl], preferred_element_type=F32)
        act_ref[:, sl] = g * jax.nn.sigmoid(g) * u
    y = x + 0.5 * jnp.dot(act_ref[...], wd_ref[...], preferred_element_type=F32)
    if final_norm:
        y = _rms_norm(y, fn_ref[...])
    o_ref[...] = y


def _const_spec(shape):
    return pl.BlockSpec(shape, lambda *_: (0,) * len(shape), pipeline_mode=pl.Buffered(1))


def _ffn(x, gain, wg, wu, wd, final_gain=None):
    m = x.shape[0]
    final_norm = final_gain is not None
    in_specs = [
        pl.BlockSpec((TM_FFN, D_MODEL), lambda i: (i, 0)),
        _const_spec((1, D_MODEL)),
        _const_spec((D_MODEL, D_FF)),
        _const_spec((D_MODEL, D_FF)),
        _const_spec((D_FF, D_MODEL)),
    ]
    args = [x, gain, wg, wu, wd]
    if final_norm:
        in_specs.append(_const_spec((1, D_MODEL)))
        args.append(final_gain)
    return pl.pallas_call(
        functools.partial(_ffn_body, final_norm=final_norm),
        grid=(m // TM_FFN,),
        in_specs=in_specs,
        out_specs=pl.BlockSpec((TM_FFN, D_MODEL), lambda i: (i, 0)),
        out_shape=jax.ShapeDtypeStruct((m, D_MODEL), F32),
        scratch_shapes=[pltpu.VMEM((TM_FFN, D_FF), F32)],
        compiler_params=pltpu.CompilerParams(
            dimension_semantics=("arbitrary",), vmem_limit_bytes=VMEM_LIMIT),
        name="ffn_final" if final_norm else "ffn",
    )(*args)


def _pair_index():
    row = lax.broadcasted_iota(jnp.int32, (PAIR, PAIR), 0)
    lane = lax.broadcasted_iota(jnp.int32, (PAIR, PAIR), 1)
    return row, lane


def _interleave(*gens):
    live = list(gens)
    while live:
        for g in list(live):
            try:
                next(g)
            except StopIteration:
                live.remove(g)


def _tri_inverse_stages(l_bds, out):
    row, lane = _pair_index()
    zero = jnp.zeros((PAIR, PAIR), F32)
    eye = jnp.where(row == lane, 1.0, 0.0).astype(F32)

    def off(m):
        return ((row // (2 * m)) == (lane // (2 * m))) & ((row % (2 * m)) >= m) & ((lane % (2 * m)) < m)

    ts = [eye + jnp.where(off(1), l, zero) for l in l_bds]
    m = 2
    while m < CHUNK:
        mask = off(m)
        tl = [_dot(t, jnp.where(mask, l, zero)) for t, l in zip(ts, l_bds)]
        yield
        ts = [t + _dot(x, t) for t, x in zip(ts, tl)]
        yield
        m *= 2
    out.extend(ts)


def _wkv_tables_stages(chunks, tril, out):
    c = CHUNK
    head0 = lax.broadcasted_iota(jnp.int32, (c, PAIR), 1) < HEAD
    row, lane = _pair_index()
    t_idx = row % c
    s_idx = lane % c
    blockdiag = (row // c) == (lane // c)
    zero = jnp.zeros((c, PAIR), F32)
    zero2 = jnp.zeros((PAIR, PAIR), F32)

    e_tots, lhs0, lhs1, rhs0, rhs1, arstk, bkh, v_swap = [], [], [], [], [], [], [], []
    for r, k, v, a, b, lw in chunks:
        h1 = lw.astype(BF)
        r1 = lw - h1.astype(F32)
        h2 = r1.astype(BF)
        h3 = (r1 - h2.astype(F32)).astype(BF)
        cs = (jnp.dot(tril, h1, preferred_element_type=F32)
              + (jnp.dot(tril, h2, preferred_element_type=F32)
                 + jnp.dot(tril, h3, preferred_element_type=F32)))
        tot = cs[c - 1:c, :]
        e_tots.append(jnp.exp(tot))
        at = a * jnp.exp(cs - lw)
        rt = r * jnp.exp(cs)
        e_n = jnp.exp(-cs)
        bt = b * e_n
        kt = k * e_n
        e_t = jnp.exp(tot - cs)
        bh = b * e_t
        kh = k * e_t
        for j in range(N_PAIR):
            sl = slice(j * PAIR, (j + 1) * PAIR)
            ar0 = jnp.concatenate([jnp.where(head0, at[:, sl], zero), jnp.where(head0, rt[:, sl], zero)], axis=0)
            ar1 = jnp.concatenate([jnp.where(head0, zero, at[:, sl]), jnp.where(head0, zero, rt[:, sl])], axis=0)
            lhs0.append(ar0)
            lhs1.append(ar1)
            rhs0.append(jnp.concatenate([bt[:, sl], kt[:, sl]], axis=0))
            rhs1.append(jnp.concatenate([kt[:, sl], bt[:, sl]], axis=0))
            arstk.append(jnp.concatenate([ar0, ar1], axis=0))
            bkh.append(jnp.concatenate([bh[:, sl], kh[:, sl]], axis=0))
            v_swap.append(jnp.concatenate([jnp.where(head0, zero, v[:, sl]),
                                           jnp.where(head0, v[:, sl], zero)], axis=0))
    yield
    g0 = [_dot_nt(x, y) for x, y in zip(lhs0, rhs0)]
    g1 = [_dot_nt(x, y) for x, y in zip(lhs1, rhs1)]
    yield
    aa = [jnp.where(s_idx < t_idx, jnp.concatenate([x[:c], y[:c]], axis=0), zero2) for x, y in zip(g0, g1)]
    arr = [jnp.where(s_idx <= t_idx, jnp.concatenate([x[c:], y[c:]], axis=0), zero2) for x, y in zip(g0, g1)]
    l_bd = [jnp.where(blockdiag, x, zero2) for x in aa]
    ak_ad = [jnp.where(blockdiag, zero2, x) for x in aa]
    t_inv = []
    yield from _tri_inverse_stages(l_bd, t_inv)
    akv = [_dot(x, y) for x, y in zip(ak_ad, v_swap)]
    yield
    for i in range(len(chunks)):
        sl = slice(i * N_PAIR, (i + 1) * N_PAIR)
        out.append(((arstk[sl], t_inv[sl], arr[sl], akv[sl], bkh[sl], v_swap[sl]), e_tots[i]))


def _wkv_apply_stages(tables, v, e_tot, states, out):
    c = CHUNK
    arstk, t_inv, arr, akv, bkh, v_swap = tables
    head0 = lax.broadcasted_iota(jnp.int32, (c, PAIR), 1) < HEAD
    row, lane = _pair_index()
    blockdiag = (row // c) == (lane // c)
    zero2 = jnp.zeros((PAIR, PAIR), F32)
    xs = [_dot_nt(x, s) for x, s in zip(arstk, states)]
    yield
    x_bd = [jnp.concatenate([z[0:c], z[2 * c:3 * c]], axis=0) for z in xs]
    rs_bd = [jnp.concatenate([z[c:2 * c], z[3 * c:4 * c]], axis=0) for z in xs]
    u_bd = [_dot(t, x + w) for t, x, w in zip(t_inv, x_bd, akv)]
    yield
    y_bd = [_dot(m, u + w) + z for m, u, w, z in zip(arr, u_bd, v_swap, rs_bd)]
    ys = [jnp.where(head0, z[:c], z[c:]) for z in y_bd]
    new_states = []
    for j in range(N_PAIR):
        sl = slice(j * PAIR, (j + 1) * PAIR)
        uv = jnp.concatenate([u_bd[j][:c] + u_bd[j][c:], v[:, sl]], axis=0)
        upd = _dot(uv.T, bkh[j])
        new_states.append(jnp.where(blockdiag, states[j] * e_tot[:, sl] + upd, zero2))
    yield
    out.append((jnp.concatenate(ys, axis=1), new_states))


def _mixer_body(x_ref, gain_ref, win_ref, convw_ref, wouta_ref, mu_ref, w0_ref, wdec_ref,
                a0_ref, wicl_ref, wgate_ref, kk_ref, ka_ref, rk_ref, lnw_ref, lnb_ref,
                woutb_ref, wo_ref, bd_ref, tril_ref,
                o_ref,
                cu_buf, pb_buf, r_s, k_s, v_s, a_s, b_s, lw_s, g_s, bon_s, y_s, m_s, st_s):
    tm = TM_MIX

    @pl.when(pl.program_id(1) == 0)
    def _():
        cu_buf[0:HALO, :] = jnp.zeros((HALO, D_CONV), F32)
        pb_buf[0:HALO, :] = jnp.zeros((HALO, COLS_B), F32)
        st_s[...] = jnp.zeros_like(st_s)

    x = x_ref[...]
    h = _rms_norm(x, gain_ref[...]).astype(BF)
    bd = bd_ref[...]

    def head_sum(z):
        half = D_RWKV // 2
        s = _dot_exact_rhs(jnp.concatenate([z[:, :half], z[:, half:]], axis=0), bd)
        return jnp.concatenate([s[:tm], s[tm:]], axis=1)

    def proj(lo, hi):
        return jnp.dot(h, win_ref[:, lo:hi], preferred_element_type=F32)

    pa_parts = []

    def branch_a_proj_stages():
        for p in range(COLS_A // D_CONV):
            pa_parts.append(proj(p * D_CONV, (p + 1) * D_CONV))
            yield

    def branch_b_prep_stages():
        pb = proj(OFF_B, OFF_GA)
        yield
        pb_buf[HALO:HALO + tm, :] = pb
        prev = pb_buf[pl.ds(HALO - 1, tm), :]
        pb_buf[0:HALO, :] = pb_buf[tm:tm + HALO, :]
        pbm = pb + (prev - pb) * mu_ref[...]
        r = pbm[:, 0:D_RWKV]
        k = pbm[:, D_RWKV:2 * D_RWKV]
        v = pbm[:, 2 * D_RWKV:3 * D_RWKV]
        xwa = pbm[:, 3 * D_RWKV:3 * D_RWKV + 128]
        xg = pbm[:, 3 * D_RWKV + 128:COLS_B]
        z = w0_ref[...] + _dot3(jnp.tanh(xwa), wdec_ref[...])
        nz = -z
        softplus = jnp.maximum(nz, 0.0) + jnp.log1p(jnp.exp(-jnp.abs(nz)))
        w_log = -softplus - 0.5
        lw_s[...] = -jnp.exp(w_log)
        iclr = jax.nn.sigmoid(a0_ref[...] + _dot3(xwa, wicl_ref[...]))
        g_s[...] = _dot(jax.nn.sigmoid(xg), wgate_ref[...])
        yield
        kk = k * kk_ref[...]
        kk = kk / jnp.maximum(jnp.sqrt(head_sum(kk * kk)), 1e-12)
        k2 = k * (1.0 + (iclr - 1.0) * ka_ref[...])
        r_s[...] = r
        k_s[...] = k2
        v_s[...] = v
        a_s[...] = -kk
        b_s[...] = kk * iclr
        yield
        bon_s[...] = head_sum(r * k2 * rk_ref[...]) * v
        yield

    _interleave(branch_b_prep_stages(), branch_a_proj_stages())

    cu = pa_parts[1] * pa_parts[2]
    cu_buf[HALO:HALO + tm, :] = cu
    cw = convw_ref[...]
    conv = (cw[2:3] * cu + cw[1:2] * cu_buf[pl.ds(HALO - 1, tm), :]
            + cw[0:1] * cu_buf[pl.ds(HALO - 2, tm), :])
    cu_buf[0:HALO, :] = cu_buf[tm:tm + HALO, :]
    gated = (pa_parts[0] * conv).astype(BF)

    def branch_a_out_stages():
        for q in range(D_MODEL // GATE_COLS):
            cols = slice(q * GATE_COLS, (q + 1) * GATE_COLS)
            ya = jnp.dot(gated, wouta_ref[:, cols], preferred_element_type=F32)
            ga = proj(OFF_GA + q * GATE_COLS, OFF_GA + (q + 1) * GATE_COLS)
            m_s[:, cols] = jax.nn.sigmoid(ga) * ya
            yield

    sgb_parts = []

    def gate_b_stages():
        for q in range(D_MODEL // GATE_COLS):
            gb = proj(OFF_GB + q * GATE_COLS, OFF_GB + (q + 1) * GATE_COLS)
            sgb_parts.append(jax.nn.sigmoid(gb))
            for _ in range(3):
                yield

    tril = tril_ref[...]
    n_group = tm // CHUNK // CHUNK_GROUP

    def chunk_rows(ci):
        return slice(ci * CHUNK, (ci + 1) * CHUNK)

    group_tables = [[] for _ in range(n_group)]
    states = [[st_s[j] for j in range(N_PAIR)]]

    def tables_stages(gi):
        chunks = []
        for ci in range(gi * CHUNK_GROUP, (gi + 1) * CHUNK_GROUP):
            rows = chunk_rows(ci)
            chunks.append((r_s[rows, :], k_s[rows, :], v_s[rows, :], a_s[rows, :], b_s[rows, :],
                           lw_s[rows, :]))
        yield from _wkv_tables_stages(chunks, tril, group_tables[gi])

    def apply_stages(gi):
        for i in range(CHUNK_GROUP):
            rows = chunk_rows(gi * CHUNK_GROUP + i)
            tables, e_tot = group_tables[gi][i]
            res = []
            yield from _wkv_apply_stages(tables, v_s[rows, :], e_tot, states[0], res)
            y_s[rows, :] = res[0][0]
            states[0] = res[0][1]

    _interleave(tables_stages(0), branch_a_out_stages())
    for gi in range(n_group):
        if gi + 1 < n_group:
            _interleave(tables_stages(gi + 1), apply_stages(gi))
        else:
            _interleave(apply_stages(gi), gate_b_stages())
    for j in range(N_PAIR):
        st_s[j] = states[0][j]

    y = y_s[...]
    mean = head_sum(y) * (1.0 / HEAD)
    d = y - mean
    var = head_sum(d * d) * (1.0 / HEAD)
    yn = d * lax.rsqrt(var + GN_EPS) * lnw_ref[...] + lnb_ref[...] + bon_s[...]
    yb = _dot(yn * g_s[...], woutb_ref[...])
    merged = m_s[...] + jnp.concatenate(sgb_parts, axis=1) * yb
    o_ref[...] = x + _dot(merged, wo_ref[...])


def _mixer(x, n_batch, gain, win, convw, wouta, mu, w0, wdec, a0, wicl, wgate, kk, ka, rk,
           lnw, lnb, woutb, wo):
    m = x.shape[0]
    tiles = m // n_batch // TM_MIX
    head_of = jnp.arange(D_RWKV // 2, dtype=jnp.int32) // HEAD
    bd = (head_of[:, None] == head_of[None, :]).astype(BF)
    idx = jnp.arange(CHUNK, dtype=jnp.int32)
    tril = (idx[None, :] <= idx[:, None]).astype(BF)
    consts = [gain, win, convw, wouta, mu, w0, wdec, a0, wicl, wgate, kk, ka, rk, lnw, lnb,
              woutb, wo, bd, tril]
    tok_spec = pl.BlockSpec((TM_MIX, D_MODEL), lambda bi, ti: (bi * tiles + ti, 0))
    vec = lambda n: pltpu.VMEM((TM_MIX, n), F32)
    return pl.pallas_call(
        _mixer_body,
        grid=(n_batch, tiles),
        in_specs=[tok_spec] + [_const_spec(c.shape) for c in consts],
        out_specs=tok_spec,
        out_shape=jax.ShapeDtypeStruct((m, D_MODEL), F32),
        scratch_shapes=[
            pltpu.VMEM((TM_MIX + HALO, D_CONV), F32),
            pltpu.VMEM((TM_MIX + HALO, COLS_B), F32),
            vec(D_RWKV), vec(D_RWKV), vec(D_RWKV), vec(D_RWKV), vec(D_RWKV), vec(D_RWKV),
            vec(D_RWKV), vec(D_RWKV), vec(D_RWKV), vec(D_MODEL),
            pltpu.VMEM((N_PAIR, PAIR, PAIR), F32),
        ],
        compiler_params=pltpu.CompilerParams(
            dimension_semantics=("arbitrary", "arbitrary"), vmem_limit_bytes=VMEM_LIMIT),
        name="mixer",
    )(x, *consts)


def kernel(x, ffn1_norm, ffn1_w_gate, ffn1_w_up, ffn1_w_down, mix_norm, w_in, conv_w, w_out_a, mu_b, w0, w_decay_up, a0, w_iclr_up, w_gate_up, k_k, k_a, r_k, ln_x_w, ln_x_b, w_out_b, w_o, ffn2_norm, ffn2_w_gate, ffn2_w_up, ffn2_w_down, final_norm):
    n_batch, seq, d = x.shape
    assert d == D_MODEL and seq % TM_MIX == 0 and (n_batch * seq) % TM_FFN == 0
    assert ffn1_norm.shape[0] == 1, "single layer"
    row = lambda t: t.reshape(1, -1).astype(F32)
    bf = lambda t: t.astype(BF)
    xf = x.reshape(n_batch * seq, d)

    x1 = _ffn(xf, row(ffn1_norm[0]), ffn1_w_gate[0], ffn1_w_up[0], ffn1_w_down[0])

    zeros_lora = jnp.zeros((64, D_RWKV), F32)
    wdec = jnp.concatenate([w_decay_up[0], zeros_lora], axis=0)
    wicl = jnp.concatenate([zeros_lora, w_iclr_up[0]], axis=0)
    x2 = _mixer(x1, n_batch, row(mix_norm[0]), bf(w_in[0]), conv_w[0].astype(F32),
                bf(w_out_a[0]), row(mu_b[0]), row(w0[0]), wdec, row(a0[0]), wicl,
                bf(w_gate_up[0]), row(k_k[0]), row(k_a[0]), row(r_k[0]), row(ln_x_w[0]),
                row(ln_x_b[0]), bf(w_out_b[0]), bf(w_o[0]))

    out = _ffn(x2, row(ffn2_norm[0]), ffn2_w_gate[0], ffn2_w_up[0], ffn2_w_down[0],
               final_gain=row(final_norm))
    return out.reshape(n_batch, seq, d)
```

```python
import functools

import jax
import jax.numpy as jnp
from jax import lax
from jax.experimental import pallas as pl
from jax.experimental.pallas import tpu as pltpu

F32 = jnp.float32
BF = jnp.bfloat16

D_MODEL = 1024
D_CONV = 512
D_RWKV = 512
HEAD = 64
D_FF = 2816
COLS_A = 3 * D_CONV
COLS_B = 3 * D_RWKV + 64 + 64 + 128
OFF_B = COLS_A
OFF_GA = COLS_A + COLS_B
OFF_GB = OFF_GA + D_MODEL
RMS_EPS = 1e-6
GN_EPS = 64e-5

CHUNK = 64
PAIR = 2 * HEAD
N_PAIR = D_RWKV // PAIR
HALO = 8
TM_FFN = 512
TF_FFN = 256
TM_MIX = 512
CHUNK_GROUP = 4
GATE_COLS = 256
VMEM_LIMIT = 56 * 1024 * 1024


def _dot(a, b):
    return jnp.dot(a.astype(BF), b.astype(BF), preferred_element_type=F32)


def _dot_nt(a, b):
    return lax.dot_general(a.astype(BF), b.astype(BF), (((1,), (1,)), ((), ())),
                           preferred_element_type=F32)


def _split2(x):
    hi = x.astype(BF)
    lo = (x - hi.astype(F32)).astype(BF)
    return hi, lo


def _dot3(a, b):
    ah, al = _split2(a)
    bh, bl = _split2(b)
    return (jnp.dot(jnp.concatenate([ah, al], axis=1), jnp.concatenate([bh, bh], axis=0),
                    preferred_element_type=F32)
            + jnp.dot(ah, bl, preferred_element_type=F32))


def _dot_exact_rhs(a, b_bf):
    ah, al = _split2(a)
    return (jnp.dot(ah, b_bf, preferred_element_type=F32)
            + jnp.dot(al, b_bf, preferred_element_type=F32))


def _rms_norm(x, gain):
    return x * lax.rsqrt(jnp.mean(x * x, axis=-1, keepdims=True) + RMS_EPS) * gain


def _ffn_body(*refs, final_norm):
    if final_norm:
        x_ref, gain_ref, wg_ref, wu_ref, wd_ref, fn_ref, o_ref, act_ref = refs
    else:
        x_ref, gain_ref, wg_ref, wu_ref, wd_ref, o_ref, act_ref = refs
    x = x_ref[...]
    h = _rms_norm(x, gain_ref[...])
    for c in range(D_FF // TF_FFN):
        sl = slice(c * TF_FFN, (c + 1) * TF_FFN)
        g = jnp.dot(h, wg_ref[:, sl], preferred_element_type=F32)
        u = jnp.dot(h, wu_ref[:, sl], preferred_element_type=F32)
        act_ref[:, sl] = g * jax.nn.sigmoid(g) * u
    y = x + 0.5 * jnp.dot(act_ref[...], wd_ref[...], preferred_element_type=F32)
    if final_norm:
        y = _rms_norm(y, fn_ref[...])
    o_ref[...] = y


def _const_spec(shape):
    return pl.BlockSpec(shape, lambda *_: (0,) * len(shape), pipeline_mode=pl.Buffered(1))


def _ffn(x, gain, wg, wu, wd, final_gain=None):
    m = x.shape[0]
    final_norm = final_gain is not None
    in_specs = [
        pl.BlockSpec((TM_FFN, D_MODEL), lambda i: (i, 0)),
        _const_spec((1, D_MODEL)),
        _const_spec((D_MODEL, D_FF)),
        _const_spec((D_MODEL, D_FF)),
        _const_spec((D_FF, D_MODEL)),
    ]
    args = [x, gain, wg, wu, wd]
    if final_norm:
        in_specs.append(_const_spec((1, D_MODEL)))
        args.append(final_gain)
    return pl.pallas_call(
        functools.partial(_ffn_body, final_norm=final_norm),
        grid=(m // TM_FFN,),
        in_specs=in_specs,
        out_specs=pl.BlockSpec((TM_FFN, D_MODEL), lambda i: (i, 0)),
        out_shape=jax.ShapeDtypeStruct((m, D_MODEL), F32),
        scratch_shapes=[pltpu.VMEM((TM_FFN, D_FF), F32)],
        compiler_params=pltpu.CompilerParams(
            dimension_semantics=("arbitrary",), vmem_limit_bytes=VMEM_LIMIT),
        name="ffn_final" if final_norm else "ffn",
    )(*args)


def _pair_index():
    row = lax.broadcasted_iota(jnp.int32, (PAIR, PAIR), 0)
    lane = lax.broadcasted_iota(jnp.int32, (PAIR, PAIR), 1)
    return row, lane


def _interleave(*gens):
    live = list(gens)
    while live:
        for g in list(live):
            try:
                next(g)
            except StopIteration:
                live.remove(g)


def _tri_inverse_stages(l_bds, out):
    row, lane = _pair_index()
    zero = jnp.zeros((PAIR, PAIR), F32)
    eye = jnp.where(row == lane, 1.0, 0.0).astype(F32)

    def off(m):
        return ((row // (2 * m)) == (lane // (2 * m))) & ((row % (2 * m)) >= m) & ((lane % (2 * m)) < m)

    ts = [eye + jnp.where(off(1), l, zero) for l in l_bds]
    m = 2
    while m < CHUNK:
        mask = off(m)
        tl = [_dot(t, jnp.where(mask, l, zero)) for t, l in zip(ts, l_bds)]
        yield
        ts = [t + _dot(x, t) for t, x in zip(ts, tl)]
        yield
        m *= 2
    out.extend(ts)


def _wkv_tables_stages(chunks, tril, out):
    c = CHUNK
    head0 = lax.broadcasted_iota(jnp.int32, (c, PAIR), 1) < HEAD
    row, lane = _pair_index()
    t_idx = row % c
    s_idx = lane % c
    blockdiag = (row // c) == (lane // c)
    zero = jnp.zeros((c, PAIR), F32)
    zero2 = jnp.zeros((PAIR, PAIR), F32)

    e_tots, lhs0, lhs1, rhs0, rhs1, arstk, bkh, v_swap = [], [], [], [], [], [], [], []
    for r, k, v, a, b, lw in chunks:
        h1 = lw.astype(BF)
        r1 = lw - h1.astype(F32)
        h2 = r1.astype(BF)
        h3 = (r1 - h2.astype(F32)).astype(BF)
        cs = jnp.dot(tril, jnp.concatenate([h1, h2, h3], axis=0), preferred_element_type=F32)
        tot = cs[c - 1:c, :]
        e_tots.append(jnp.exp(tot))
        at = a * jnp.exp(cs - lw)
        rt = r * jnp.exp(cs)
        e_n = jnp.exp(-cs)
        bt = b * e_n
        kt = k * e_n
        e_t = jnp.exp(tot - cs)
        bh = b * e_t
        kh = k * e_t
        for j in range(N_PAIR):
            sl = slice(j * PAIR, (j + 1) * PAIR)
            ar0 = jnp.concatenate([jnp.where(head0, at[:, sl], zero), jnp.where(head0, rt[:, sl], zero)], axis=0)
            ar1 = jnp.concatenate([jnp.where(head0, zero, at[:, sl]), jnp.where(head0, zero, rt[:, sl])], axis=0)
            lhs0.append(ar0)
            lhs1.append(ar1)
            rhs0.append(jnp.concatenate([bt[:, sl], kt[:, sl]], axis=0))
            rhs1.append(jnp.concatenate([kt[:, sl], bt[:, sl]], axis=0))
            arstk.append(jnp.concatenate([ar0, ar1], axis=0))
            bkh.append(jnp.concatenate([bh[:, sl], kh[:, sl]], axis=0))
            v_swap.append(jnp.concatenate([jnp.where(head0, zero, v[:, sl]),
                                           jnp.where(head0, v[:, sl], zero)], axis=0))
    yield
    g0 = [_dot_nt(x, y) for x, y in zip(lhs0, rhs0)]
    g1 = [_dot_nt(x, y) for x, y in zip(lhs1, rhs1)]
    yield
    aa = [jnp.where(s_idx < t_idx, jnp.concatenate([x[:c], y[:c]], axis=0), zero2) for x, y in zip(g0, g1)]
    arr = [jnp.where(s_idx <= t_idx, jnp.concatenate([x[c:], y[c:]], axis=0), zero2) for x, y in zip(g0, g1)]
    l_bd = [jnp.where(blockdiag, x, zero2) for x in aa]
    ak_ad = [jnp.where(blockdiag, zero2, x) for x in aa]
    t_inv = []
    yield from _tri_inverse_stages(l_bd, t_inv)
    akv = [_dot(x, y) for x, y in zip(ak_ad, v_swap)]
    yield
    for i in range(len(chunks)):
        sl = slice(i * N_PAIR, (i + 1) * N_PAIR)
        out.append(((arstk[sl], t_inv[sl], arr[sl], akv[sl], bkh[sl], v_swap[sl]), e_tots[i]))


def _wkv_apply_stages(tables, v, e_tot, states, out):
    c = CHUNK
    arstk, t_inv, arr, akv, bkh, v_swap = tables
    head0 = lax.broadcasted_iota(jnp.int32, (c, PAIR), 1) < HEAD
    row, lane = _pair_index()
    blockdiag = (row // c) == (lane // c)
    zero2 = jnp.zeros((PAIR, PAIR), F32)
    xs = [_dot_nt(x, s) for x, s in zip(arstk, states)]
    yield
    x_bd = [jnp.concatenate([z[0:c], z[2 * c:3 * c]], axis=0) for z in xs]
    rs_bd = [jnp.concatenate([z[c:2 * c], z[3 * c:4 * c]], axis=0) for z in xs]
    u_bd = [_dot(t, x + w) for t, x, w in zip(t_inv, x_bd, akv)]
    yield
    y_bd = [_dot(m, u + w) + z for m, u, w, z in zip(arr, u_bd, v_swap, rs_bd)]
    ys = [jnp.where(head0, z[:c], z[c:]) for z in y_bd]
    new_states = []
    for j in range(N_PAIR):
        sl = slice(j * PAIR, (j + 1) * PAIR)
        uv = jnp.concatenate([u_bd[j][:c] + u_bd[j][c:], v[:, sl]], axis=0)
        upd = _dot(uv.T, bkh[j])
        new_states.append(jnp.where(blockdiag, states[j] * e_tot[:, sl] + upd, zero2))
    yield
    out.append((jnp.concatenate(ys, axis=1), new_states))


def _mixer_body(x_ref, gain_ref, win_ref, convw_ref, wouta_ref, mu_ref, w0_ref, wdec_ref,
                a0_ref, wicl_ref, wgate_ref, kk_ref, ka_ref, rk_ref, lnw_ref, lnb_ref,
                woutb_ref, wo_ref, bd_ref, tril_ref,
                o_ref,
                cu_buf, pb_buf, r_s, k_s, v_s, a_s, b_s, lw_s, g_s, bon_s, y_s, m_s, st_s):
    tm = TM_MIX

    @pl.when(pl.program_id(1) == 0)
    def _():
        cu_buf[0:HALO, :] = jnp.zeros((HALO, D_CONV), F32)
        pb_buf[0:HALO, :] = jnp.zeros((HALO, COLS_B), F32)
        st_s[...] = jnp.zeros_like(st_s)

    x = x_ref[...]
    h = _rms_norm(x, gain_ref[...]).astype(BF)
    bd = bd_ref[...]

    def head_sum(z):
        half = D_RWKV // 2
        n = z.shape[0]
        s = _dot_exact_rhs(jnp.concatenate([z[:, :half], z[:, half:]], axis=0), bd)
        return jnp.concatenate([s[:n], s[n:]], axis=1)

    def proj(lo, hi):
        return jnp.dot(h, win_ref[:, lo:hi], preferred_element_type=F32)

    pa_parts = []

    def branch_a_proj_stages():
        for p in range(COLS_A // D_CONV):
            pa_parts.append(proj(p * D_CONV, (p + 1) * D_CONV))
            yield

    def branch_b_prep_stages():
        pb = proj(OFF_B, OFF_GA)
        yield
        pb_buf[HALO:HALO + tm, :] = pb
        prev = pb_buf[pl.ds(HALO - 1, tm), :]
        pb_buf[0:HALO, :] = pb_buf[tm:tm + HALO, :]
        pbm = pb + (prev - pb) * mu_ref[...]
        r = pbm[:, 0:D_RWKV]
        k = pbm[:, D_RWKV:2 * D_RWKV]
        v = pbm[:, 2 * D_RWKV:3 * D_RWKV]
        xwa = pbm[:, 3 * D_RWKV:3 * D_RWKV + 128]
        xg = pbm[:, 3 * D_RWKV + 128:COLS_B]
        z = w0_ref[...] + _dot3(jnp.tanh(xwa), wdec_ref[...])
        nz = -z
        softplus = jnp.maximum(nz, 0.0) + jnp.log1p(jnp.exp(-jnp.abs(nz)))
        w_log = -softplus - 0.5
        lw_s[...] = -jnp.exp(w_log)
        iclr = jax.nn.sigmoid(a0_ref[...] + _dot3(xwa, wicl_ref[...]))
        g_s[...] = _dot(jax.nn.sigmoid(xg), wgate_ref[...])
        yield
        kk = k * kk_ref[...]
        kk = kk / jnp.maximum(jnp.sqrt(head_sum(kk * kk)), 1e-12)
        k2 = k * (1.0 + (iclr - 1.0) * ka_ref[...])
        r_s[...] = r
        k_s[...] = k2
        v_s[...] = v
        a_s[...] = -kk
        b_s[...] = kk * iclr
        yield
        bon_s[...] = head_sum(r * k2 * rk_ref[...]) * v
        yield

    _interleave(branch_b_prep_stages(), branch_a_proj_stages())

    cu = pa_parts[1] * pa_parts[2]
    cu_buf[HALO:HALO + tm, :] = cu
    cw = convw_ref[...]
    conv = (cw[2:3] * cu + cw[1:2] * cu_buf[pl.ds(HALO - 1, tm), :]
            + cw[0:1] * cu_buf[pl.ds(HALO - 2, tm), :])
    cu_buf[0:HALO, :] = cu_buf[tm:tm + HALO, :]
    gated = (pa_parts[0] * conv).astype(BF)

    def branch_a_out_stages():
        for q in range(D_MODEL // GATE_COLS):
            cols = slice(q * GATE_COLS, (q + 1) * GATE_COLS)
            ya = jnp.dot(gated, wouta_ref[:, cols], preferred_element_type=F32)
            ga = proj(OFF_GA + q * GATE_COLS, OFF_GA + (q + 1) * GATE_COLS)
            m_s[:, cols] = jax.nn.sigmoid(ga) * ya
            yield

    group_rows = CHUNK_GROUP * CHUNK
    n_group = tm // group_rows
    sgb_parts = [[] for _ in range(n_group)]

    def gate_b_stages(gi):
        rows = slice(gi * group_rows, (gi + 1) * group_rows)
        for q in range(D_MODEL // GATE_COLS):
            lo = OFF_GB + q * GATE_COLS
            gb = jnp.dot(h[rows], win_ref[:, lo:lo + GATE_COLS], preferred_element_type=F32)
            sgb_parts[gi].append(jax.nn.sigmoid(gb))
            yield

    def post_stages(gi):
        rows = slice(gi * group_rows, (gi + 1) * group_rows)
        y = y_s[rows, :]
        mean = head_sum(y) * (1.0 / HEAD)
        yield
        d = y - mean
        var = head_sum(d * d) * (1.0 / HEAD)
        yield
        yn = d * lax.rsqrt(var + GN_EPS) * lnw_ref[...] + lnb_ref[...] + bon_s[rows, :]
        yb = _dot(yn * g_s[rows, :], woutb_ref[...])
        yield
        merged = m_s[rows, :] + jnp.concatenate(sgb_parts[gi], axis=1) * yb
        o_ref[rows, :] = x[rows] + _dot(merged, wo_ref[...])
        yield

    tril = tril_ref[...]

    def chunk_rows(ci):
        return slice(ci * CHUNK, (ci + 1) * CHUNK)

    group_tables = [[] for _ in range(n_group)]
    states = [[st_s[j] for j in range(N_PAIR)]]

    def tables_stages(gi):
        chunks = []
        for ci in range(gi * CHUNK_GROUP, (gi + 1) * CHUNK_GROUP):
            rows = chunk_rows(ci)
            chunks.append((r_s[rows, :], k_s[rows, :], v_s[rows, :], a_s[rows, :], b_s[rows, :],
                           lw_s[rows, :]))
        yield from _wkv_tables_stages(chunks, tril, group_tables[gi])

    def apply_stages(gi):
        for i in range(CHUNK_GROUP):
            rows = chunk_rows(gi * CHUNK_GROUP + i)
            tables, e_tot = group_tables[gi][i]
            res = []
            yield from _wkv_apply_stages(tables, v_s[rows, :], e_tot, states[0], res)
            y_s[rows, :] = res[0][0]
            states[0] = res[0][1]

    _interleave(tables_stages(0), branch_a_out_stages())
    for gi in range(n_group):
        stages = [apply_stages(gi), gate_b_stages(gi)]
        if gi + 1 < n_group:
            stages.insert(0, tables_stages(gi + 1))
        if gi > 0:
            stages.append(post_stages(gi - 1))
        _interleave(*stages)
    for j in range(N_PAIR):
        st_s[j] = states[0][j]
    _interleave(post_stages(n_group - 1))


def _mixer(x, n_batch, gain, win, convw, wouta, mu, w0, wdec, a0, wicl, wgate, kk, ka, rk,
           lnw, lnb, woutb, wo):
    m = x.shape[0]
    tiles = m // n_batch // TM_MIX
    head_of = jnp.arange(D_RWKV // 2, dtype=jnp.int32) // HEAD
    bd = (head_of[:, None] == head_of[None, :]).astype(BF)
    idx = jnp.arange(CHUNK, dtype=jnp.int32)
    tril = (idx[None, :] <= idx[:, None]).astype(BF)
    tril = jnp.concatenate([tril, tril, tril], axis=1)
    consts = [gain, win, convw, wouta, mu, w0, wdec, a0, wicl, wgate, kk, ka, rk, lnw, lnb,
              woutb, wo, bd, tril]
    tok_spec = pl.BlockSpec((TM_MIX, D_MODEL), lambda bi, ti: (bi * tiles + ti, 0))
    vec = lambda n: pltpu.VMEM((TM_MIX, n), F32)
    return pl.pallas_call(
        _mixer_body,
        grid=(n_batch, tiles),
        in_specs=[tok_spec] + [_const_spec(c.shape) for c in consts],
        out_specs=tok_spec,
        out_shape=jax.ShapeDtypeStruct((m, D_MODEL), F32),
        scratch_shapes=[
            pltpu.VMEM((TM_MIX + HALO, D_CONV), F32),
            pltpu.VMEM((TM_MIX + HALO, COLS_B), F32),
            vec(D_RWKV), vec(D_RWKV), vec(D_RWKV), vec(D_RWKV), vec(D_RWKV), vec(D_RWKV),
            vec(D_RWKV), vec(D_RWKV), vec(D_RWKV), vec(D_MODEL),
            pltpu.VMEM((N_PAIR, PAIR, PAIR), F32),
        ],
        compiler_params=pltpu.CompilerParams(
            dimension_semantics=("arbitrary", "arbitrary"), vmem_limit_bytes=VMEM_LIMIT),
        name="mixer",
    )(x, *consts)


def kernel(x, ffn1_norm, ffn1_w_gate, ffn1_w_up, ffn1_w_down, mix_norm, w_in, conv_w, w_out_a, mu_b, w0, w_decay_up, a0, w_iclr_up, w_gate_up, k_k, k_a, r_k, ln_x_w, ln_x_b, w_out_b, w_o, ffn2_norm, ffn2_w_gate, ffn2_w_up, ffn2_w_down, final_norm):
    n_batch, seq, d = x.shape
    assert d == D_MODEL and seq % TM_MIX == 0 and (n_batch * seq) % TM_FFN == 0
    assert ffn1_norm.shape[0] == 1, "single layer"
    row = lambda t: t.reshape(1, -1).astype(F32)
    bf = lambda t: t.astype(BF)
    xf = x.reshape(n_batch * seq, d)

    x1 = _ffn(xf, row(ffn1_norm[0]), ffn1_w_gate[0], ffn1_w_up[0], ffn1_w_down[0])

    zeros_lora = jnp.zeros((64, D_RWKV), F32)
    wdec = jnp.concatenate([w_decay_up[0], zeros_lora], axis=0)
    wicl = jnp.concatenate([zeros_lora, w_iclr_up[0]], axis=0)
    x2 = _mixer(x1, n_batch, row(mix_norm[0]), bf(w_in[0]), conv_w[0].astype(F32),
                bf(w_out_a[0]), row(mu_b[0]), row(w0[0]), wdec, row(a0[0]), wicl,
                bf(w_gate_up[0]), row(k_k[0]), row(k_a[0]), row(r_k[0]), row(ln_x_w[0]),
                row(ln_x_b[0]), bf(w_out_b[0]), bf(w_o[0]))

    out = _ffn(x2, row(ffn2_norm[0]), ffn2_w_gate[0], ffn2_w_up[0], ffn2_w_down[0],
               final_gain=row(final_norm))
    return out.reshape(n_batch, seq, d)
```

```python
import functools

import jax
import jax.numpy as jnp
from jax import lax
from jax.experimental import pallas as pl
from jax.experimental.pallas import tpu as pltpu

F32 = jnp.float32
BF = jnp.bfloat16

D_MODEL = 1024
D_CONV = 512
D_RWKV = 512
HEAD = 64
D_FF = 2816
COLS_A = 3 * D_CONV
COLS_B = 3 * D_RWKV + 64 + 64 + 128
OFF_B = COLS_A
OFF_GA = COLS_A + COLS_B
OFF_GB = OFF_GA + D_MODEL
RMS_EPS = 1e-6
GN_EPS = 64e-5

CHUNK = 64
PAIR = 2 * HEAD
N_PAIR = D_RWKV // PAIR
HALO = 8
TM_FFN = 512
TF_FFN = 256
TM_MIX = 512
CHUNK_GROUP = 4
GATE_COLS = 256
VMEM_LIMIT = 56 * 1024 * 1024


def _dot(a, b):
    return jnp.dot(a.astype(BF), b.astype(BF), preferred_element_type=F32)


def _dot_nt(a, b):
    return lax.dot_general(a.astype(BF), b.astype(BF), (((1,), (1,)), ((), ())),
                           preferred_element_type=F32)


def _dot_f32(a, b):
    return jnp.dot(a, b, preferred_element_type=F32)


def _split2(x):
    hi = x.astype(BF)
    lo = (x - hi.astype(F32)).astype(BF)
    return hi, lo


def _dot3(a, b):
    ah, al = _split2(a)
    bh, bl = _split2(b)
    return (jnp.dot(jnp.concatenate([ah, al], axis=1), jnp.concatenate([bh, bh], axis=0),
                    preferred_element_type=F32)
            + jnp.dot(ah, bl, preferred_element_type=F32))


def _dot_exact_rhs(a, b_bf):
    ah, al = _split2(a)
    return (jnp.dot(ah, b_bf, preferred_element_type=F32)
            + jnp.dot(al, b_bf, preferred_element_type=F32))


def _rms_norm(x, gain):
    return x * lax.rsqrt(jnp.mean(x * x, axis=-1, keepdims=True) + RMS_EPS) * gain


def _ffn_body(*refs, final_norm):
    if final_norm:
        x_ref, gain_ref, wg_ref, wu_ref, wd_ref, fn_ref, o_ref, act_ref = refs
    else:
        x_ref, gain_ref, wg_ref, wu_ref, wd_ref, o_ref, act_ref = refs
    x = x_ref[...]
    h = _rms_norm(x, gain_ref[...])
    for c in range(D_FF // TF_FFN):
        sl = slice(c * TF_FFN, (c + 1) * TF_FFN)
        g = _dot_f32(h, wg_ref[:, sl])
        u = _dot_f32(h, wu_ref[:, sl])
        act_ref[:, sl] = g * jax.nn.sigmoid(g) * u
    y = x + 0.5 * _dot_f32(act_ref[...], wd_ref[...])
    if final_norm:
        y = _rms_norm(y, fn_ref[...])
    o_ref[...] = y


def _const_spec(shape):
    return pl.BlockSpec(shape, lambda *_: (0,) * len(shape), pipeline_mode=pl.Buffered(1))


def _ffn(x, gain, wg, wu, wd, final_gain=None):
    m = x.shape[0]
    final_norm = final_gain is not None
    in_specs = [
        pl.BlockSpec((TM_FFN, D_MODEL), lambda i: (i, 0)),
        _const_spec((1, D_MODEL)),
        _const_spec((D_MODEL, D_FF)),
        _const_spec((D_MODEL, D_FF)),
        _const_spec((D_FF, D_MODEL)),
    ]
    args = [x, gain, wg, wu, wd]
    if final_norm:
        in_specs.append(_const_spec((1, D_MODEL)))
        args.append(final_gain)
    return pl.pallas_call(
        functools.partial(_ffn_body, final_norm=final_norm),
        grid=(m // TM_FFN,),
        in_specs=in_specs,
        out_specs=pl.BlockSpec((TM_FFN, D_MODEL), lambda i: (i, 0)),
        out_shape=jax.ShapeDtypeStruct((m, D_MODEL), F32),
        scratch_shapes=[pltpu.VMEM((TM_FFN, D_FF), F32)],
        compiler_params=pltpu.CompilerParams(
            dimension_semantics=("arbitrary",), vmem_limit_bytes=VMEM_LIMIT),
        name="ffn_final" if final_norm else "ffn",
    )(*args)


def _pair_index():
    row = lax.broadcasted_iota(jnp.int32, (PAIR, PAIR), 0)
    lane = lax.broadcasted_iota(jnp.int32, (PAIR, PAIR), 1)
    return row, lane


def _interleave(*gens):
    live = list(gens)
    while live:
        for g in list(live):
            try:
                next(g)
            except StopIteration:
                live.remove(g)


def _tri_inverse_stages(l_bds, out):
    row, lane = _pair_index()
    zero = jnp.zeros((PAIR, PAIR), F32)
    eye = jnp.where(row == lane, 1.0, 0.0).astype(F32)

    def off(m):
        return ((row // (2 * m)) == (lane // (2 * m))) & ((row % (2 * m)) >= m) & ((lane % (2 * m)) < m)

    ts = [eye + jnp.where(off(1), l, zero) for l in l_bds]
    m = 2
    while m < CHUNK:
        mask = off(m)
        tl = [_dot(t, jnp.where(mask, l, zero)) for t, l in zip(ts, l_bds)]
        yield
        ts = [t + _dot(x, t) for t, x in zip(ts, tl)]
        yield
        m *= 2
    out.extend(ts)


def _wkv_tables_stages(chunks, tril, out):
    c = CHUNK
    head0 = lax.broadcasted_iota(jnp.int32, (c, PAIR), 1) < HEAD
    row, lane = _pair_index()
    t_idx = row % c
    s_idx = lane % c
    blockdiag = (row // c) == (lane // c)
    zero = jnp.zeros((c, PAIR), F32)
    zero2 = jnp.zeros((PAIR, PAIR), F32)

    e_tots, lhs0, lhs1, rhs0, rhs1, arstk, bkh, v_swap = [], [], [], [], [], [], [], []
    for r, k, v, a, b, lw in chunks:
        h1 = lw.astype(BF)
        r1 = lw - h1.astype(F32)
        h2 = r1.astype(BF)
        h3 = (r1 - h2.astype(F32)).astype(BF)
        cs = jnp.dot(tril, jnp.concatenate([h1, h2, h3], axis=0), preferred_element_type=F32)
        tot = cs[c - 1:c, :]
        e_tots.append(jnp.exp(tot))
        at = a * jnp.exp(cs - lw)
        rt = r * jnp.exp(cs)
        e_n = jnp.exp(-cs)
        bt = b * e_n
        kt = k * e_n
        e_t = jnp.exp(tot - cs)
        bh = b * e_t
        kh = k * e_t
        for j in range(N_PAIR):
            sl = slice(j * PAIR, (j + 1) * PAIR)
            ar0 = jnp.concatenate([jnp.where(head0, at[:, sl], zero), jnp.where(head0, rt[:, sl], zero)], axis=0)
            ar1 = jnp.concatenate([jnp.where(head0, zero, at[:, sl]), jnp.where(head0, zero, rt[:, sl])], axis=0)
            lhs0.append(ar0)
            lhs1.append(ar1)
            rhs0.append(jnp.concatenate([bt[:, sl], kt[:, sl]], axis=0))
            rhs1.append(jnp.concatenate([kt[:, sl], bt[:, sl]], axis=0))
            arstk.append(jnp.concatenate([ar0, ar1], axis=0))
            bkh.append(jnp.concatenate([bh[:, sl], kh[:, sl]], axis=0))
            v_swap.append(jnp.concatenate([jnp.where(head0, zero, v[:, sl]),
                                           jnp.where(head0, v[:, sl], zero)], axis=0))
    yield
    g0 = [_dot_nt(x, y) for x, y in zip(lhs0, rhs0)]
    g1 = [_dot_nt(x, y) for x, y in zip(lhs1, rhs1)]
    yield
    aa = [jnp.where(s_idx < t_idx, jnp.concatenate([x[:c], y[:c]], axis=0), zero2) for x, y in zip(g0, g1)]
    arr = [jnp.where(s_idx <= t_idx, jnp.concatenate([x[c:], y[c:]], axis=0), zero2) for x, y in zip(g0, g1)]
    l_bd = [jnp.where(blockdiag, x, zero2) for x in aa]
    ak_ad = [jnp.where(blockdiag, zero2, x) for x in aa]
    t_inv = []
    yield from _tri_inverse_stages(l_bd, t_inv)
    akv = [_dot(x, y) for x, y in zip(ak_ad, v_swap)]
    yield
    for i in range(len(chunks)):
        sl = slice(i * N_PAIR, (i + 1) * N_PAIR)
        out.append(((arstk[sl], t_inv[sl], arr[sl], akv[sl], bkh[sl], v_swap[sl]), e_tots[i]))


def _wkv_apply_stages(tables, v, e_tot, states, out):
    c = CHUNK
    arstk, t_inv, arr, akv, bkh, v_swap = tables
    head0 = lax.broadcasted_iota(jnp.int32, (c, PAIR), 1) < HEAD
    row, lane = _pair_index()
    blockdiag = (row // c) == (lane // c)
    zero2 = jnp.zeros((PAIR, PAIR), F32)
    xs = [_dot_nt(x, s) for x, s in zip(arstk, states)]
    yield
    x_bd = [jnp.concatenate([z[0:c], z[2 * c:3 * c]], axis=0) for z in xs]
    rs_bd = [jnp.concatenate([z[c:2 * c], z[3 * c:4 * c]], axis=0) for z in xs]
    u_bd = [_dot(t, x + w) for t, x, w in zip(t_inv, x_bd, akv)]
    yield
    y_bd = [_dot(m, u + w) + z for m, u, w, z in zip(arr, u_bd, v_swap, rs_bd)]
    ys = [jnp.where(head0, z[:c], z[c:]) for z in y_bd]
    new_states = []
    for j in range(N_PAIR):
        sl = slice(j * PAIR, (j + 1) * PAIR)
        uv = jnp.concatenate([u_bd[j][:c] + u_bd[j][c:], v[:, sl]], axis=0)
        upd = _dot(uv.T, bkh[j])
        new_states.append(jnp.where(blockdiag, states[j] * e_tot[:, sl] + upd, zero2))
    yield
    out.append((jnp.concatenate(ys, axis=1), new_states))


def _mixer_body(x_ref, gain_ref, win_ref, convw_ref, wouta_ref, mu_ref, w0_ref, wdec_ref,
                a0_ref, wicl_ref, wgate_ref, kk_ref, ka_ref, rk_ref, lnw_ref, lnb_ref,
                woutb_ref, wo_ref, bd_ref, tril_ref,
                o_ref,
                cu_buf, pb_buf, r_s, k_s, v_s, a_s, b_s, lw_s, g_s, bon_s, y_s, m_s, st_s):
    tm = TM_MIX

    @pl.when(pl.program_id(1) == 0)
    def _():
        cu_buf[0:HALO, :] = jnp.zeros((HALO, D_CONV), F32)
        pb_buf[0:HALO, :] = jnp.zeros((HALO, COLS_B), F32)
        st_s[...] = jnp.zeros_like(st_s)

    x = x_ref[...]
    h = _rms_norm(x, gain_ref[...]).astype(BF)
    bd = bd_ref[...]

    def head_sum(z):
        half = D_RWKV // 2
        n = z.shape[0]
        s = _dot_exact_rhs(jnp.concatenate([z[:, :half], z[:, half:]], axis=0), bd)
        return jnp.concatenate([s[:n], s[n:]], axis=1)

    def proj(lo, hi):
        return jnp.dot(h, win_ref[:, lo:hi], preferred_element_type=F32)

    pa_parts = []

    def branch_a_proj_stages():
        for p in range(COLS_A // D_CONV):
            pa_parts.append(proj(p * D_CONV, (p + 1) * D_CONV))
            yield

    def branch_b_prep_stages():
        pb = proj(OFF_B, OFF_GA)
        yield
        pb_buf[HALO:HALO + tm, :] = pb
        prev = pb_buf[pl.ds(HALO - 1, tm), :]
        pb_buf[0:HALO, :] = pb_buf[tm:tm + HALO, :]
        pbm = pb + (prev - pb) * mu_ref[...]
        r = pbm[:, 0:D_RWKV]
        k = pbm[:, D_RWKV:2 * D_RWKV]
        v = pbm[:, 2 * D_RWKV:3 * D_RWKV]
        xwa = pbm[:, 3 * D_RWKV:3 * D_RWKV + 128]
        xg = pbm[:, 3 * D_RWKV + 128:COLS_B]
        z = w0_ref[...] + _dot3(jnp.tanh(xwa), wdec_ref[...])
        nz = -z
        softplus = jnp.maximum(nz, 0.0) + jnp.log1p(jnp.exp(-jnp.abs(nz)))
        w_log = -softplus - 0.5
        lw_s[...] = -jnp.exp(w_log)
        iclr = jax.nn.sigmoid(a0_ref[...] + _dot3(xwa, wicl_ref[...]))
        g_s[...] = _dot_f32(jax.nn.sigmoid(xg), wgate_ref[...])
        yield
        kk = k * kk_ref[...]
        kk = kk * lax.rsqrt(jnp.maximum(head_sum(kk * kk), 1e-24))
        k2 = k * (1.0 + (iclr - 1.0) * ka_ref[...])
        r_s[...] = r
        k_s[...] = k2
        v_s[...] = v
        a_s[...] = -kk
        b_s[...] = kk * iclr
        yield
        bon_s[...] = head_sum(r * k2 * rk_ref[...]) * v
        yield

    _interleave(branch_b_prep_stages(), branch_a_proj_stages())

    cu = pa_parts[1] * pa_parts[2]
    cu_buf[HALO:HALO + tm, :] = cu
    cw = convw_ref[...]
    conv = (cw[2:3] * cu + cw[1:2] * cu_buf[pl.ds(HALO - 1, tm), :]
            + cw[0:1] * cu_buf[pl.ds(HALO - 2, tm), :])
    cu_buf[0:HALO, :] = cu_buf[tm:tm + HALO, :]
    gated = pa_parts[0] * conv

    def branch_a_out_stages():
        for q in range(D_MODEL // GATE_COLS):
            cols = slice(q * GATE_COLS, (q + 1) * GATE_COLS)
            ya = _dot_f32(gated, wouta_ref[:, cols])
            ga = proj(OFF_GA + q * GATE_COLS, OFF_GA + (q + 1) * GATE_COLS)
            m_s[:, cols] = jax.nn.sigmoid(ga) * ya
            yield

    group_rows = CHUNK_GROUP * CHUNK
    n_group = tm // group_rows
    sgb_parts = [[] for _ in range(n_group)]

    def gate_b_stages(gi):
        rows = slice(gi * group_rows, (gi + 1) * group_rows)
        for q in range(D_MODEL // GATE_COLS):
            lo = OFF_GB + q * GATE_COLS
            gb = jnp.dot(h[rows], win_ref[:, lo:lo + GATE_COLS], preferred_element_type=F32)
            sgb_parts[gi].append(jax.nn.sigmoid(gb))
            yield

    def post_stages(gi):
        rows = slice(gi * group_rows, (gi + 1) * group_rows)
        y = y_s[rows, :]
        mean = head_sum(y) * (1.0 / HEAD)
        yield
        d = y - mean
        var = head_sum(d * d) * (1.0 / HEAD)
        yield
        yn = d * lax.rsqrt(var + GN_EPS) * lnw_ref[...] + lnb_ref[...] + bon_s[rows, :]
        yb = _dot_f32(yn * g_s[rows, :], woutb_ref[...])
        yield
        merged = m_s[rows, :] + jnp.concatenate(sgb_parts[gi], axis=1) * yb
        o_ref[rows, :] = x[rows] + _dot_f32(merged, wo_ref[...])
        yield

    tril = tril_ref[...]

    def chunk_rows(ci):
        return slice(ci * CHUNK, (ci + 1) * CHUNK)

    group_tables = [[] for _ in range(n_group)]
    states = [[st_s[j] for j in range(N_PAIR)]]

    def tables_stages(gi):
        chunks = []
        for ci in range(gi * CHUNK_GROUP, (gi + 1) * CHUNK_GROUP):
            rows = chunk_rows(ci)
            chunks.append((r_s[rows, :], k_s[rows, :], v_s[rows, :], a_s[rows, :], b_s[rows, :],
                           lw_s[rows, :]))
        yield from _wkv_tables_stages(chunks, tril, group_tables[gi])

    def apply_stages(gi):
        for i in range(CHUNK_GROUP):
            rows = chunk_rows(gi * CHUNK_GROUP + i)
            tables, e_tot = group_tables[gi][i]
            res = []
            yield from _wkv_apply_stages(tables, v_s[rows, :], e_tot, states[0], res)
            y_s[rows, :] = res[0][0]
            states[0] = res[0][1]

    _interleave(tables_stages(0), branch_a_out_stages())
    for gi in range(n_group):
        stages = [apply_stages(gi), gate_b_stages(gi)]
        if gi + 1 < n_group:
            stages.insert(0, tables_stages(gi + 1))
        if gi > 0:
            stages.append(post_stages(gi - 1))
        _interleave(*stages)
    for j in range(N_PAIR):
        st_s[j] = states[0][j]
    _interleave(post_stages(n_group - 1))


def _mixer(x, n_batch, gain, win, convw, wouta, mu, w0, wdec, a0, wicl, wgate, kk, ka, rk,
           lnw, lnb, woutb, wo):
    m = x.shape[0]
    tiles = m // n_batch // TM_MIX
    head_of = jnp.arange(D_RWKV // 2, dtype=jnp.int32) // HEAD
    bd = (head_of[:, None] == head_of[None, :]).astype(BF)
    idx = jnp.arange(CHUNK, dtype=jnp.int32)
    tril = (idx[None, :] <= idx[:, None]).astype(BF)
    tril = jnp.concatenate([tril, tril, tril], axis=1)
    consts = [gain, win, convw, wouta, mu, w0, wdec, a0, wicl, wgate, kk, ka, rk, lnw, lnb,
              woutb, wo, bd, tril]
    tok_spec = pl.BlockSpec((TM_MIX, D_MODEL), lambda bi, ti: (bi * tiles + ti, 0))
    vec = lambda n: pltpu.VMEM((TM_MIX, n), F32)
    return pl.pallas_call(
        _mixer_body,
        grid=(n_batch, tiles),
        in_specs=[tok_spec] + [_const_spec(c.shape) for c in consts],
        out_specs=tok_spec,
        out_shape=jax.ShapeDtypeStruct((m, D_MODEL), F32),
        scratch_shapes=[
            pltpu.VMEM((TM_MIX + HALO, D_CONV), F32),
            pltpu.VMEM((TM_MIX + HALO, COLS_B), F32),
            vec(D_RWKV), vec(D_RWKV), vec(D_RWKV), vec(D_RWKV), vec(D_RWKV), vec(D_RWKV),
            vec(D_RWKV), vec(D_RWKV), vec(D_RWKV),
            vec(D_MODEL),
            pltpu.VMEM((N_PAIR, PAIR, PAIR), F32),
        ],
        compiler_params=pltpu.CompilerParams(
            dimension_semantics=("arbitrary", "arbitrary"), vmem_limit_bytes=VMEM_LIMIT),
        name="mixer",
    )(x, *consts)


def kernel(x, ffn1_norm, ffn1_w_gate, ffn1_w_up, ffn1_w_down, mix_norm, w_in, conv_w, w_out_a, mu_b, w0, w_decay_up, a0, w_iclr_up, w_gate_up, k_k, k_a, r_k, ln_x_w, ln_x_b, w_out_b, w_o, ffn2_norm, ffn2_w_gate, ffn2_w_up, ffn2_w_down, final_norm):
    n_batch, seq, d = x.shape
    assert d == D_MODEL and seq % TM_MIX == 0 and (n_batch * seq) % TM_FFN == 0
    assert ffn1_norm.shape[0] == 1, "single layer"
    row = lambda t: t.reshape(1, -1).astype(F32)
    xf = x.reshape(n_batch * seq, d)

    x1 = _ffn(xf, row(ffn1_norm[0]), ffn1_w_gate[0], ffn1_w_up[0], ffn1_w_down[0])

    zeros_lora = jnp.zeros((64, D_RWKV), F32)
    wdec = jnp.concatenate([w_decay_up[0], zeros_lora], axis=0)
    wicl = jnp.concatenate([zeros_lora, w_iclr_up[0]], axis=0)
    x2 = _mixer(x1, n_batch, row(mix_norm[0]), w_in[0].astype(BF), conv_w[0].astype(F32),
                w_out_a[0], row(mu_b[0]), row(w0[0]), wdec, row(a0[0]), wicl,
                w_gate_up[0], row(k_k[0]), row(k_a[0]), row(r_k[0]), row(ln_x_w[0]),
                row(ln_x_b[0]), w_out_b[0], w_o[0])

    out = _ffn(x2, row(ffn2_norm[0]), ffn2_w_gate[0], ffn2_w_up[0], ffn2_w_down[0],
               final_gain=row(final_norm))
    return out.reshape(n_batch, seq, d)
```

```python
import functools

import jax
import jax.numpy as jnp
from jax import lax
from jax.experimental import pallas as pl
from jax.experimental.pallas import tpu as pltpu

F32 = jnp.float32
BF = jnp.bfloat16

D_MODEL = 1024
D_CONV = 512
D_RWKV = 512
HEAD = 64
D_FF = 2816
COLS_A = 3 * D_CONV
COLS_B = 3 * D_RWKV + 64 + 64 + 128
OFF_B = COLS_A
OFF_GA = COLS_A + COLS_B
OFF_GB = OFF_GA + D_MODEL
RMS_EPS = 1e-6
GN_EPS = 64e-5

CHUNK = 64
PAIR = 2 * HEAD
N_PAIR = D_RWKV // PAIR
HALO = 8
TM_FFN = 512
TF_FFN = 256
TM_MIX = 512
CHUNK_GROUP = 4
GATE_COLS = 256
VMEM_LIMIT = 56 * 1024 * 1024


def _dot(a, b):
    return jnp.dot(a.astype(BF), b.astype(BF), preferred_element_type=F32)


def _dot_nt(a, b):
    return lax.dot_general(a.astype(BF), b.astype(BF), (((1,), (1,)), ((), ())),
                           preferred_element_type=F32)


def _dot_f32(a, b):
    return jnp.dot(a, b, preferred_element_type=F32)


def _split2(x):
    hi = x.astype(BF)
    lo = (x - hi.astype(F32)).astype(BF)
    return hi, lo


def _dot3(a, b):
    ah, al = _split2(a)
    bh, bl = _split2(b)
    return (jnp.dot(jnp.concatenate([ah, al], axis=1), jnp.concatenate([bh, bh], axis=0),
                    preferred_element_type=F32)
            + jnp.dot(ah, bl, preferred_element_type=F32))


def _dot_exact_rhs(a, b_bf):
    ah, al = _split2(a)
    return (jnp.dot(ah, b_bf, preferred_element_type=F32)
            + jnp.dot(al, b_bf, preferred_element_type=F32))


def _rms_norm(x, gain):
    return x * lax.rsqrt(jnp.mean(x * x, axis=-1, keepdims=True) + RMS_EPS) * gain


def _ffn_body(*refs, final_norm):
    if final_norm:
        x_ref, gain_ref, wg_ref, wu_ref, wd_ref, fn_ref, o_ref, act_ref = refs
    else:
        x_ref, gain_ref, wg_ref, wu_ref, wd_ref, o_ref, act_ref = refs
    x = x_ref[...]
    h = _rms_norm(x, gain_ref[...])
    for c in range(D_FF // TF_FFN):
        sl = slice(c * TF_FFN, (c + 1) * TF_FFN)
        g = _dot_f32(h, wg_ref[:, sl])
        u = _dot_f32(h, wu_ref[:, sl])
        act_ref[:, sl] = g * jax.nn.sigmoid(g) * u
    y = x + 0.5 * _dot_f32(act_ref[...], wd_ref[...])
    if final_norm:
        y = _rms_norm(y, fn_ref[...])
    o_ref[...] = y


def _const_spec(shape):
    return pl.BlockSpec(shape, lambda *_: (0,) * len(shape), pipeline_mode=pl.Buffered(1))


def _ffn(x, gain, wg, wu, wd, final_gain=None):
    m = x.shape[0]
    final_norm = final_gain is not None
    in_specs = [
        pl.BlockSpec((TM_FFN, D_MODEL), lambda i: (i, 0)),
        _const_spec((1, D_MODEL)),
        _const_spec((D_MODEL, D_FF)),
        _const_spec((D_MODEL, D_FF)),
        _const_spec((D_FF, D_MODEL)),
    ]
    args = [x, gain, wg, wu, wd]
    if final_norm:
        in_specs.append(_const_spec((1, D_MODEL)))
        args.append(final_gain)
    return pl.pallas_call(
        functools.partial(_ffn_body, final_norm=final_norm),
        grid=(m // TM_FFN,),
        in_specs=in_specs,
        out_specs=pl.BlockSpec((TM_FFN, D_MODEL), lambda i: (i, 0)),
        out_shape=jax.ShapeDtypeStruct((m, D_MODEL), F32),
        scratch_shapes=[pltpu.VMEM((TM_FFN, D_FF), F32)],
        compiler_params=pltpu.CompilerParams(
            dimension_semantics=("arbitrary",), vmem_limit_bytes=VMEM_LIMIT),
        name="ffn_final" if final_norm else "ffn",
    )(*args)


def _pair_index():
    row = lax.broadcasted_iota(jnp.int32, (PAIR, PAIR), 0)
    lane = lax.broadcasted_iota(jnp.int32, (PAIR, PAIR), 1)
    return row, lane


def _interleave(*gens):
    live = list(gens)
    while live:
        for g in list(live):
            try:
                next(g)
            except StopIteration:
                live.remove(g)


def _tri_inverse_stages(l_bds, out):
    row, lane = _pair_index()
    zero = jnp.zeros((PAIR, PAIR), F32)
    eye = jnp.where(row == lane, 1.0, 0.0).astype(F32)

    def off(m):
        return ((row // (2 * m)) == (lane // (2 * m))) & ((row % (2 * m)) >= m) & ((lane % (2 * m)) < m)

    ts = [eye + jnp.where(off(1), l, zero) for l in l_bds]
    m = 2
    while m < CHUNK:
        mask = off(m)
        tl = [_dot(t, jnp.where(mask, l, zero)) for t, l in zip(ts, l_bds)]
        yield
        ts = [t + _dot(x, t) for t, x in zip(ts, tl)]
        yield
        m *= 2
    out.extend(ts)


def _wkv_tables_stages(chunks, tril, out):
    c = CHUNK
    head0 = lax.broadcasted_iota(jnp.int32, (c, PAIR), 1) < HEAD
    row, lane = _pair_index()
    t_idx = row % c
    s_idx = lane % c
    blockdiag = (row // c) == (lane // c)
    zero = jnp.zeros((c, PAIR), F32)
    zero2 = jnp.zeros((PAIR, PAIR), F32)

    e_tots, lhs0, lhs1, rhs0, rhs1, at_bd, rt_bd, bkh, v_swap = [], [], [], [], [], [], [], [], []
    for r, k, v, a, b, lw in chunks:
        h1 = lw.astype(BF)
        r1 = lw - h1.astype(F32)
        h2 = r1.astype(BF)
        h3 = (r1 - h2.astype(F32)).astype(BF)
        cs = jnp.dot(tril, jnp.concatenate([h1, h2, h3], axis=0), preferred_element_type=F32)
        tot = cs[c - 1:c, :]
        e_tots.append(jnp.exp(tot))
        at = a * jnp.exp(cs - lw)
        rt = r * jnp.exp(cs)
        e_n = jnp.exp(-cs)
        bt = b * e_n
        kt = k * e_n
        e_t = jnp.exp(tot - cs)
        bh = b * e_t
        kh = k * e_t
        for j in range(N_PAIR):
            sl = slice(j * PAIR, (j + 1) * PAIR)
            ar0 = jnp.concatenate([jnp.where(head0, at[:, sl], zero), jnp.where(head0, rt[:, sl], zero)], axis=0)
            ar1 = jnp.concatenate([jnp.where(head0, zero, at[:, sl]), jnp.where(head0, zero, rt[:, sl])], axis=0)
            lhs0.append(ar0)
            lhs1.append(ar1)
            rhs0.append(jnp.concatenate([bt[:, sl], kt[:, sl]], axis=0))
            rhs1.append(jnp.concatenate([kt[:, sl], bt[:, sl]], axis=0))
            at_bd.append(jnp.concatenate([ar0[:c], ar1[:c]], axis=0))
            rt_bd.append(jnp.concatenate([ar0[c:], ar1[c:]], axis=0))
            bkh.append(jnp.concatenate([bh[:, sl], kh[:, sl]], axis=0))
            v_swap.append(jnp.concatenate([jnp.where(head0, zero, v[:, sl]),
                                           jnp.where(head0, v[:, sl], zero)], axis=0))
    yield
    g0 = [_dot_nt(x, y) for x, y in zip(lhs0, rhs0)]
    g1 = [_dot_nt(x, y) for x, y in zip(lhs1, rhs1)]
    yield
    aa = [jnp.where(s_idx < t_idx, jnp.concatenate([x[:c], y[:c]], axis=0), zero2) for x, y in zip(g0, g1)]
    arr = [jnp.where(s_idx <= t_idx, jnp.concatenate([x[c:], y[c:]], axis=0), zero2) for x, y in zip(g0, g1)]
    l_bd = [jnp.where(blockdiag, x, zero2) for x in aa]
    ak_ad = [jnp.where(blockdiag, zero2, x) for x in aa]
    akv = [_dot(x, y) for x, y in zip(ak_ad, v_swap)]
    yield
    t_inv = []
    yield from _tri_inverse_stages(l_bd, t_inv)
    t_at_akv = [_dot(t, jnp.concatenate([x, w], axis=1)) for t, x, w in zip(t_inv, at_bd, akv)]
    yield
    tar = [jnp.concatenate([z[:, :PAIR], y], axis=0) for z, y in zip(t_at_akv, rt_bd)]
    u_v = [z[:, PAIR:] for z in t_at_akv]
    for i in range(len(chunks)):
        sl = slice(i * N_PAIR, (i + 1) * N_PAIR)
        out.append(((tar[sl], u_v[sl], arr[sl], bkh[sl], v_swap[sl]), e_tots[i]))


def _wkv_apply_stages(tables, v, e_tot, states, out):
    c = CHUNK
    tar, u_v, arr, bkh, v_swap = tables
    head0 = lax.broadcasted_iota(jnp.int32, (c, PAIR), 1) < HEAD
    row, lane = _pair_index()
    blockdiag = (row // c) == (lane // c)
    zero2 = jnp.zeros((PAIR, PAIR), F32)
    xs = [_dot_nt(x, s) for x, s in zip(tar, states)]
    yield
    u_bd = [z[:PAIR] + w for z, w in zip(xs, u_v)]
    rs_bd = [z[PAIR:] for z in xs]
    y_bd = [_dot(m, u + w) + z for m, u, w, z in zip(arr, u_bd, v_swap, rs_bd)]
    ys = [jnp.where(head0, z[:c], z[c:]) for z in y_bd]
    new_states = []
    for j in range(N_PAIR):
        sl = slice(j * PAIR, (j + 1) * PAIR)
        uv = jnp.concatenate([u_bd[j][:c] + u_bd[j][c:], v[:, sl]], axis=0)
        upd = _dot(uv.T, bkh[j])
        new_states.append(jnp.where(blockdiag, states[j] * e_tot[:, sl] + upd, zero2))
    yield
    out.append((jnp.concatenate(ys, axis=1), new_states))


def _mixer_body(x_ref, gain_ref, win_ref, convw_ref, wouta_ref, mu_ref, w0_ref, wdec_ref,
                a0_ref, wicl_ref, wgate_ref, kk_ref, ka_ref, rk_ref, lnw_ref, lnb_ref,
                woutb_ref, wo_ref, bd_ref, tril_ref,
                o_ref,
                cu_buf, pb_buf, r_s, k_s, v_s, a_s, b_s, lw_s, g_s, bon_s, y_s, m_s, st_s):
    tm = TM_MIX

    @pl.when(pl.program_id(1) == 0)
    def _():
        cu_buf[0:HALO, :] = jnp.zeros((HALO, D_CONV), F32)
        pb_buf[0:HALO, :] = jnp.zeros((HALO, COLS_B), F32)
        st_s[...] = jnp.zeros_like(st_s)

    x = x_ref[...]
    h = _rms_norm(x, gain_ref[...]).astype(BF)
    bd = bd_ref[...]

    def head_sum(z):
        half = D_RWKV // 2
        n = z.shape[0]
        s = _dot_exact_rhs(jnp.concatenate([z[:, :half], z[:, half:]], axis=0), bd)
        return jnp.concatenate([s[:n], s[n:]], axis=1)

    def proj(lo, hi):
        return jnp.dot(h, win_ref[:, lo:hi], preferred_element_type=F32)

    pa_parts = []

    def branch_a_proj_stages():
        for p in range(COLS_A // D_CONV):
            pa_parts.append(proj(p * D_CONV, (p + 1) * D_CONV))
            yield

    def branch_b_prep_stages():
        pb = proj(OFF_B, OFF_GA)
        yield
        pb_buf[HALO:HALO + tm, :] = pb
        prev = pb_buf[pl.ds(HALO - 1, tm), :]
        pb_buf[0:HALO, :] = pb_buf[tm:tm + HALO, :]
        pbm = pb + (prev - pb) * mu_ref[...]
        r = pbm[:, 0:D_RWKV]
        k = pbm[:, D_RWKV:2 * D_RWKV]
        v = pbm[:, 2 * D_RWKV:3 * D_RWKV]
        xwa = pbm[:, 3 * D_RWKV:3 * D_RWKV + 128]
        xg = pbm[:, 3 * D_RWKV + 128:COLS_B]
        z = w0_ref[...] + _dot3(jnp.tanh(xwa), wdec_ref[...])
        nz = -z
        softplus = jnp.maximum(nz, 0.0) + jnp.log1p(jnp.exp(-jnp.abs(nz)))
        w_log = -softplus - 0.5
        lw_s[...] = -jnp.exp(w_log)
        iclr = jax.nn.sigmoid(a0_ref[...] + _dot3(xwa, wicl_ref[...]))
        g_s[...] = _dot_f32(jax.nn.sigmoid(xg), wgate_ref[...])
        yield
        kk = k * kk_ref[...]
        kk = kk * lax.rsqrt(jnp.maximum(head_sum(kk * kk), 1e-24))
        k2 = k * (1.0 + (iclr - 1.0) * ka_ref[...])
        r_s[...] = r
        k_s[...] = k2
        v_s[...] = v
        a_s[...] = -kk
        b_s[...] = kk * iclr
        yield
        bon_s[...] = head_sum(r * k2 * rk_ref[...]) * v
        yield

    _interleave(branch_b_prep_stages(), branch_a_proj_stages())

    cu = pa_parts[1] * pa_parts[2]
    cu_buf[HALO:HALO + tm, :] = cu
    cw = convw_ref[...]
    conv = (cw[2:3] * cu + cw[1:2] * cu_buf[pl.ds(HALO - 1, tm), :]
            + cw[0:1] * cu_buf[pl.ds(HALO - 2, tm), :])
    cu_buf[0:HALO, :] = cu_buf[tm:tm + HALO, :]
    gated = pa_parts[0] * conv

    def branch_a_out_stages():
        for q in range(D_MODEL // GATE_COLS):
            cols = slice(q * GATE_COLS, (q + 1) * GATE_COLS)
            ya = _dot_f32(gated, wouta_ref[:, cols])
            ga = proj(OFF_GA + q * GATE_COLS, OFF_GA + (q + 1) * GATE_COLS)
            m_s[:, cols] = jax.nn.sigmoid(ga) * ya
            yield

    group_rows = CHUNK_GROUP * CHUNK
    n_group = tm // group_rows
    sgb_parts = [[] for _ in range(n_group)]

    def gate_b_stages(gi):
        rows = slice(gi * group_rows, (gi + 1) * group_rows)
        for q in range(D_MODEL // GATE_COLS):
            lo = OFF_GB + q * GATE_COLS
            gb = jnp.dot(h[rows], win_ref[:, lo:lo + GATE_COLS], preferred_element_type=F32)
            sgb_parts[gi].append(jax.nn.sigmoid(gb))
            yield

    def post_stages(gi):
        rows = slice(gi * group_rows, (gi + 1) * group_rows)
        y = y_s[rows, :]
        mean = head_sum(y) * (1.0 / HEAD)
        yield
        d = y - mean
        var = head_sum(d * d) * (1.0 / HEAD)
        yield
        yn = d * lax.rsqrt(var + GN_EPS) * lnw_ref[...] + lnb_ref[...] + bon_s[rows, :]
        yb = _dot_f32(yn * g_s[rows, :], woutb_ref[...])
        yield
        merged = m_s[rows, :] + jnp.concatenate(sgb_parts[gi], axis=1) * yb
        o_ref[rows, :] = x[rows] + _dot_f32(merged, wo_ref[...])
        yield

    tril = tril_ref[...]

    def chunk_rows(ci):
        return slice(ci * CHUNK, (ci + 1) * CHUNK)

    group_tables = [[] for _ in range(n_group)]
    states = [[st_s[j] for j in range(N_PAIR)]]

    def tables_stages(gi):
        chunks = []
        for ci in range(gi * CHUNK_GROUP, (gi + 1) * CHUNK_GROUP):
            rows = chunk_rows(ci)
            chunks.append((r_s[rows, :], k_s[rows, :], v_s[rows, :], a_s[rows, :], b_s[rows, :],
                           lw_s[rows, :]))
        yield from _wkv_tables_stages(chunks, tril, group_tables[gi])

    def apply_stages(gi):
        for i in range(CHUNK_GROUP):
            rows = chunk_rows(gi * CHUNK_GROUP + i)
            tables, e_tot = group_tables[gi][i]
            res = []
            yield from _wkv_apply_stages(tables, v_s[rows, :], e_tot, states[0], res)
            y_s[rows, :] = res[0][0]
            states[0] = res[0][1]

    _interleave(tables_stages(0), branch_a_out_stages())
    for gi in range(n_group):
        stages = [apply_stages(gi), gate_b_stages(gi)]
        if gi + 1 < n_group:
            stages.insert(0, tables_stages(gi + 1))
        if gi > 0:
            stages.append(post_stages(gi - 1))
        _interleave(*stages)
    for j in range(N_PAIR):
        st_s[j] = states[0][j]
    _interleave(post_stages(n_group - 1))


def _mixer(x, n_batch, gain, win, convw, wouta, mu, w0, wdec, a0, wicl, wgate, kk, ka, rk,
           lnw, lnb, woutb, wo):
    m = x.shape[0]
    tiles = m // n_batch // TM_MIX
    head_of = jnp.arange(D_RWKV // 2, dtype=jnp.int32) // HEAD
    bd = (head_of[:, None] == head_of[None, :]).astype(BF)
    idx = jnp.arange(CHUNK, dtype=jnp.int32)
    tril = (idx[None, :] <= idx[:, None]).astype(BF)
    tril = jnp.concatenate([tril, tril, tril], axis=1)
    consts = [gain, win, convw, wouta, mu, w0, wdec, a0, wicl, wgate, kk, ka, rk, lnw, lnb,
              woutb, wo, bd, tril]
    tok_spec = pl.BlockSpec((TM_MIX, D_MODEL), lambda bi, ti: (bi * tiles + ti, 0))
    vec = lambda n: pltpu.VMEM((TM_MIX, n), F32)
    return pl.pallas_call(
        _mixer_body,
        grid=(n_batch, tiles),
        in_specs=[tok_spec] + [_const_spec(c.shape) for c in consts],
        out_specs=tok_spec,
        out_shape=jax.ShapeDtypeStruct((m, D_MODEL), F32),
        scratch_shapes=[
            pltpu.VMEM((TM_MIX + HALO, D_CONV), F32),
            pltpu.VMEM((TM_MIX + HALO, COLS_B), F32),
            vec(D_RWKV), vec(D_RWKV), vec(D_RWKV), vec(D_RWKV), vec(D_RWKV), vec(D_RWKV),
            vec(D_RWKV), vec(D_RWKV), vec(D_RWKV),
            vec(D_MODEL),
            pltpu.VMEM((N_PAIR, PAIR, PAIR), F32),
        ],
        compiler_params=pltpu.CompilerParams(
            dimension_semantics=("arbitrary", "arbitrary"), vmem_limit_bytes=VMEM_LIMIT),
        name="mixer",
    )(x, *consts)


def kernel(x, ffn1_norm, ffn1_w_gate, ffn1_w_up, ffn1_w_down, mix_norm, w_in, conv_w, w_out_a, mu_b, w0, w_decay_up, a0, w_iclr_up, w_gate_up, k_k, k_a, r_k, ln_x_w, ln_x_b, w_out_b, w_o, ffn2_norm, ffn2_w_gate, ffn2_w_up, ffn2_w_down, final_norm):
    n_batch, seq, d = x.shape
    assert d == D_MODEL and seq % TM_MIX == 0 and (n_batch * seq) % TM_FFN == 0
    assert ffn1_norm.shape[0] == 1, "single layer"
    row = lambda t: t.reshape(1, -1).astype(F32)
    xf = x.reshape(n_batch * seq, d)

    x1 = _ffn(xf, row(ffn1_norm[0]), ffn1_w_gate[0], ffn1_w_up[0], ffn1_w_down[0])

    zeros_lora = jnp.zeros((64, D_RWKV), F32)
    wdec = jnp.concatenate([w_decay_up[0], zeros_lora], axis=0)
    wicl = jnp.concatenate([zeros_lora, w_iclr_up[0]], axis=0)
    x2 = _mixer(x1, n_batch, row(mix_norm[0]), w_in[0].astype(BF), conv_w[0].astype(F32),
                w_out_a[0], row(mu_b[0]), row(w0[0]), wdec, row(a0[0]), wicl,
                w_gate_up[0], row(k_k[0]), row(k_a[0]), row(r_k[0]), row(ln_x_w[0]),
                row(ln_x_b[0]), w_out_b[0], w_o[0])

    out = _ffn(x2, row(ffn2_norm[0]), ffn2_w_gate[0], ffn2_w_up[0], ffn2_w_down[0],
               final_gain=row(final_norm))
    return out.reshape(n_batch, seq, d)
```

```python
import functools

import jax
import jax.numpy as jnp
from jax import lax
from jax.experimental import pallas as pl
from jax.experimental.pallas import tpu as pltpu

F32 = jnp.float32
BF = jnp.bfloat16

D_MODEL = 1024
D_CONV = 512
D_RWKV = 512
HEAD = 64
D_FF = 2816
COLS_A = 3 * D_CONV
COLS_B = 3 * D_RWKV + 64 + 64 + 128
OFF_B = COLS_A
OFF_GA = COLS_A + COLS_B
OFF_GB = OFF_GA + D_MODEL
RMS_EPS = 1e-6
GN_EPS = 64e-5
DECAY_SCALE = 0.6065306597126334

CHUNK = 64
PAIR = 2 * HEAD
N_PAIR = D_RWKV // PAIR
HALO = 8
TM_FFN = 512
TF_FFN = 256
TM_MIX = 512
CHUNK_GROUP = 4
GATE_COLS = 256
VMEM_LIMIT = 56 * 1024 * 1024


def _dot(a, b):
    return jnp.dot(a.astype(BF), b.astype(BF), preferred_element_type=F32)


def _dot_nt(a, b):
    return lax.dot_general(a.astype(BF), b.astype(BF), (((1,), (1,)), ((), ())),
                           preferred_element_type=F32)


def _dot_f32(a, b):
    return jnp.dot(a, b, preferred_element_type=F32)


def _split2(x):
    hi = x.astype(BF)
    lo = (x - hi.astype(F32)).astype(BF)
    return hi, lo


def _dot3(a, b):
    ah, al = _split2(a)
    bh, bl = _split2(b)
    return (jnp.dot(jnp.concatenate([ah, al], axis=1), jnp.concatenate([bh, bh], axis=0),
                    preferred_element_type=F32)
            + jnp.dot(ah, bl, preferred_element_type=F32))


def _dot_exact_rhs(a, b_bf):
    ah, al = _split2(a)
    return (jnp.dot(ah, b_bf, preferred_element_type=F32)
            + jnp.dot(al, b_bf, preferred_element_type=F32))


def _rms_norm(x, gain):
    return x * lax.rsqrt(jnp.mean(x * x, axis=-1, keepdims=True) + RMS_EPS) * gain


def _ffn_body(*refs, final_norm):
    if final_norm:
        x_ref, gain_ref, wg_ref, wu_ref, wd_ref, fn_ref, o_ref, act_ref = refs
    else:
        x_ref, gain_ref, wg_ref, wu_ref, wd_ref, o_ref, act_ref = refs
    x = x_ref[...]
    h = _rms_norm(x, gain_ref[...])
    for c in range(D_FF // TF_FFN):
        sl = slice(c * TF_FFN, (c + 1) * TF_FFN)
        g = _dot_f32(h, wg_ref[:, sl])
        u = _dot_f32(h, wu_ref[:, sl])
        act_ref[:, sl] = g * jax.nn.sigmoid(g) * u
    y = x + 0.5 * _dot_f32(act_ref[...], wd_ref[...])
    if final_norm:
        y = _rms_norm(y, fn_ref[...])
    o_ref[...] = y


def _const_spec(shape):
    return pl.BlockSpec(shape, lambda *_: (0,) * len(shape), pipeline_mode=pl.Buffered(1))


def _ffn(x, gain, wg, wu, wd, final_gain=None):
    m = x.shape[0]
    final_norm = final_gain is not None
    in_specs = [
        pl.BlockSpec((TM_FFN, D_MODEL), lambda i: (i, 0)),
        _const_spec((1, D_MODEL)),
        _const_spec((D_MODEL, D_FF)),
        _const_spec((D_MODEL, D_FF)),
        _const_spec((D_FF, D_MODEL)),
    ]
    args = [x, gain, wg, wu, wd]
    if final_norm:
        in_specs.append(_const_spec((1, D_MODEL)))
        args.append(final_gain)
    return pl.pallas_call(
        functools.partial(_ffn_body, final_norm=final_norm),
        grid=(m // TM_FFN,),
        in_specs=in_specs,
        out_specs=pl.BlockSpec((TM_FFN, D_MODEL), lambda i: (i, 0)),
        out_shape=jax.ShapeDtypeStruct((m, D_MODEL), F32),
        scratch_shapes=[pltpu.VMEM((TM_FFN, D_FF), F32)],
        compiler_params=pltpu.CompilerParams(
            dimension_semantics=("arbitrary",), vmem_limit_bytes=VMEM_LIMIT),
        name="ffn_final" if final_norm else "ffn",
    )(*args)


def _pair_index():
    row = lax.broadcasted_iota(jnp.int32, (PAIR, PAIR), 0)
    lane = lax.broadcasted_iota(jnp.int32, (PAIR, PAIR), 1)
    return row, lane


def _interleave(*gens):
    live = list(gens)
    while live:
        for g in list(live):
            try:
                next(g)
            except StopIteration:
                live.remove(g)


def _tri_inverse_stages(l_bds, out):
    row, lane = _pair_index()
    zero = jnp.zeros((PAIR, PAIR), F32)
    eye = jnp.where(row == lane, 1.0, 0.0).astype(F32)

    def off(m):
        return ((row // (2 * m)) == (lane // (2 * m))) & ((row % (2 * m)) >= m) & ((lane % (2 * m)) < m)

    ts = [eye + jnp.where(off(1), l, zero) for l in l_bds]
    m = 2
    while m < CHUNK:
        mask = off(m)
        tl = [_dot(t, jnp.where(mask, l, zero)) for t, l in zip(ts, l_bds)]
        yield
        ts = [t + _dot(x, t) for t, x in zip(ts, tl)]
        yield
        m *= 2
    out.extend(ts)


def _wkv_tables_stages(chunks, tril, out):
    c = CHUNK
    head0 = lax.broadcasted_iota(jnp.int32, (c, PAIR), 1) < HEAD
    row, lane = _pair_index()
    t_idx = row % c
    s_idx = lane % c
    blockdiag = (row // c) == (lane // c)
    zero = jnp.zeros((c, PAIR), F32)
    zero2 = jnp.zeros((PAIR, PAIR), F32)

    e_tots, lhs0, lhs1, rhs0, rhs1, at_bd, rt_bd, bkh, v_swap = [], [], [], [], [], [], [], [], []
    for r, k, v, a, b, lw in chunks:
        h1 = lw.astype(BF)
        r1 = lw - h1.astype(F32)
        h2 = r1.astype(BF)
        h3 = (r1 - h2.astype(F32)).astype(BF)
        cs = jnp.dot(tril, jnp.concatenate([h1, h2, h3], axis=0), preferred_element_type=F32)
        tot = cs[c - 1:c, :]
        e_tots.append(jnp.exp(tot))
        at = a * jnp.exp(cs - lw)
        rt = r * jnp.exp(cs)
        e_n = jnp.exp(-cs)
        bt = b * e_n
        kt = k * e_n
        bh = bt * e_tots[-1]
        kh = kt * e_tots[-1]
        for j in range(N_PAIR):
            sl = slice(j * PAIR, (j + 1) * PAIR)
            ar0 = jnp.concatenate([jnp.where(head0, at[:, sl], zero), jnp.where(head0, rt[:, sl], zero)], axis=0)
            ar1 = jnp.concatenate([jnp.where(head0, zero, at[:, sl]), jnp.where(head0, zero, rt[:, sl])], axis=0)
            lhs0.append(ar0)
            lhs1.append(ar1)
            rhs0.append(jnp.concatenate([bt[:, sl], kt[:, sl]], axis=0))
            rhs1.append(jnp.concatenate([kt[:, sl], bt[:, sl]], axis=0))
            at_bd.append(jnp.concatenate([ar0[:c], ar1[:c]], axis=0))
            rt_bd.append(jnp.concatenate([ar0[c:], ar1[c:]], axis=0))
            bkh.append(jnp.concatenate([bh[:, sl], kh[:, sl]], axis=0))
            v_swap.append(jnp.concatenate([jnp.where(head0, zero, v[:, sl]),
                                           jnp.where(head0, v[:, sl], zero)], axis=0))
    yield
    g0 = [_dot_nt(x, y) for x, y in zip(lhs0, rhs0)]
    g1 = [_dot_nt(x, y) for x, y in zip(lhs1, rhs1)]
    yield
    aa = [jnp.where(s_idx < t_idx, jnp.concatenate([x[:c], y[:c]], axis=0), zero2) for x, y in zip(g0, g1)]
    arr = [jnp.where(s_idx <= t_idx, jnp.concatenate([x[c:], y[c:]], axis=0), zero2) for x, y in zip(g0, g1)]
    l_bd = [jnp.where(blockdiag, x, zero2) for x in aa]
    ak_ad = [jnp.where(blockdiag, zero2, x) for x in aa]
    akv = [_dot(x, y) for x, y in zip(ak_ad, v_swap)]
    yield
    t_inv = []
    yield from _tri_inverse_stages(l_bd, t_inv)
    t_at_akv = [_dot(t, jnp.concatenate([x, w], axis=1)) for t, x, w in zip(t_inv, at_bd, akv)]
    yield
    tar = [jnp.concatenate([z[:, :PAIR], y], axis=0) for z, y in zip(t_at_akv, rt_bd)]
    u_v = [z[:, PAIR:] for z in t_at_akv]
    for i in range(len(chunks)):
        sl = slice(i * N_PAIR, (i + 1) * N_PAIR)
        out.append(((tar[sl], u_v[sl], arr[sl], bkh[sl], v_swap[sl]), e_tots[i]))


def _wkv_apply_stages(tables, v, e_tot, states, out):
    c = CHUNK
    tar, u_v, arr, bkh, v_swap = tables
    head0 = lax.broadcasted_iota(jnp.int32, (c, PAIR), 1) < HEAD
    row, lane = _pair_index()
    blockdiag = (row // c) == (lane // c)
    zero2 = jnp.zeros((PAIR, PAIR), F32)
    xs = [_dot_nt(x, s) for x, s in zip(tar, states)]
    yield
    u_bd = [z[:PAIR] + w for z, w in zip(xs, u_v)]
    rs_bd = [z[PAIR:] for z in xs]
    y_bd = [_dot(m, u + w) + z for m, u, w, z in zip(arr, u_bd, v_swap, rs_bd)]
    ys = [jnp.where(head0, z[:c], z[c:]) for z in y_bd]
    new_states = []
    for j in range(N_PAIR):
        sl = slice(j * PAIR, (j + 1) * PAIR)
        uv = jnp.concatenate([u_bd[j][:c] + u_bd[j][c:], v[:, sl]], axis=0)
        upd = _dot(uv.T, bkh[j])
        new_states.append(jnp.where(blockdiag, states[j] * e_tot[:, sl] + upd, zero2))
    yield
    out.append((jnp.concatenate(ys, axis=1), new_states))


def _mixer_body(x_ref, gain_ref, win_ref, convw_ref, wouta_ref, mu_ref, w0_ref, wdec_ref,
                a0_ref, wicl_ref, wgate_ref, kk_ref, ka_ref, rk_ref, lnw_ref, lnb_ref,
                woutb_ref, wo_ref, bd_ref, tril_ref,
                o_ref,
                cu_buf, pb_buf, r_s, k_s, v_s, a_s, b_s, lw_s, g_s, bon_s, y_s, m_s, st_s):
    tm = TM_MIX

    @pl.when(pl.program_id(1) == 0)
    def _():
        cu_buf[0:HALO, :] = jnp.zeros((HALO, D_CONV), F32)
        pb_buf[0:HALO, :] = jnp.zeros((HALO, COLS_B), F32)
        st_s[...] = jnp.zeros_like(st_s)

    x = x_ref[...]
    h = _rms_norm(x, gain_ref[...]).astype(BF)
    bd = bd_ref[...]

    def head_sum(z):
        half = D_RWKV // 2
        n = z.shape[0]
        s = _dot_exact_rhs(jnp.concatenate([z[:, :half], z[:, half:]], axis=0), bd)
        return jnp.concatenate([s[:n], s[n:]], axis=1)

    def proj(lo, hi):
        return jnp.dot(h, win_ref[:, lo:hi], preferred_element_type=F32)

    pa_parts = []

    def branch_a_proj_stages():
        for p in range(COLS_A // D_CONV):
            pa_parts.append(proj(p * D_CONV, (p + 1) * D_CONV))
            yield

    def branch_b_prep_stages():
        pb = proj(OFF_B, OFF_GA)
        yield
        pb_buf[HALO:HALO + tm, :] = pb
        prev = pb_buf[pl.ds(HALO - 1, tm), :]
        pb_buf[0:HALO, :] = pb_buf[tm:tm + HALO, :]
        pbm = pb + (prev - pb) * mu_ref[...]
        r = pbm[:, 0:D_RWKV]
        k = pbm[:, D_RWKV:2 * D_RWKV]
        v = pbm[:, 2 * D_RWKV:3 * D_RWKV]
        xwa = pbm[:, 3 * D_RWKV:3 * D_RWKV + 128]
        xg = pbm[:, 3 * D_RWKV + 128:COLS_B]
        z = w0_ref[...] + _dot3(jnp.tanh(xwa), wdec_ref[...])
        lw_s[...] = -DECAY_SCALE * jax.nn.sigmoid(z)
        iclr = jax.nn.sigmoid(a0_ref[...] + _dot3(xwa, wicl_ref[...]))
        g_s[...] = _dot_f32(jax.nn.sigmoid(xg), wgate_ref[...])
        yield
        kk = k * kk_ref[...]
        kk = kk * lax.rsqrt(jnp.maximum(head_sum(kk * kk), 1e-24))
        k2 = k * (1.0 + (iclr - 1.0) * ka_ref[...])
        r_s[...] = r
        k_s[...] = k2
        v_s[...] = v
        a_s[...] = -kk
        b_s[...] = kk * iclr
        yield
        bon_s[...] = head_sum(r * k2 * rk_ref[...]) * v
        yield

    _interleave(branch_b_prep_stages(), branch_a_proj_stages())

    cu = pa_parts[1] * pa_parts[2]
    cu_buf[HALO:HALO + tm, :] = cu
    cw = convw_ref[...]
    conv = (cw[2:3] * cu + cw[1:2] * cu_buf[pl.ds(HALO - 1, tm), :]
            + cw[0:1] * cu_buf[pl.ds(HALO - 2, tm), :])
    cu_buf[0:HALO, :] = cu_buf[tm:tm + HALO, :]
    gated = pa_parts[0] * conv

    def branch_a_out_stages():
        for q in range(D_MODEL // GATE_COLS):
            cols = slice(q * GATE_COLS, (q + 1) * GATE_COLS)
            ya = _dot_f32(gated, wouta_ref[:, cols])
            ga = proj(OFF_GA + q * GATE_COLS, OFF_GA + (q + 1) * GATE_COLS)
            m_s[:, cols] = jax.nn.sigmoid(ga) * ya
            yield

    group_rows = CHUNK_GROUP * CHUNK
    n_group = tm // group_rows
    sgb_parts = [[] for _ in range(n_group)]

    def gate_b_stages(gi):
        rows = slice(gi * group_rows, (gi + 1) * group_rows)
        for q in range(D_MODEL // GATE_COLS):
            lo = OFF_GB + q * GATE_COLS
            gb = jnp.dot(h[rows], win_ref[:, lo:lo + GATE_COLS], preferred_element_type=F32)
            sgb_parts[gi].append(jax.nn.sigmoid(gb))
            yield

    def post_stages(gi):
        rows = slice(gi * group_rows, (gi + 1) * group_rows)
        y = y_s[rows, :]
        mean = head_sum(y) * (1.0 / HEAD)
        yield
        d = y - mean
        var = head_sum(d * d) * (1.0 / HEAD)
        yield
        yn = d * lax.rsqrt(var + GN_EPS) * lnw_ref[...] + lnb_ref[...] + bon_s[rows, :]
        yb = _dot_f32(yn * g_s[rows, :], woutb_ref[...])
        yield
        merged = m_s[rows, :] + jnp.concatenate(sgb_parts[gi], axis=1) * yb
        o_ref[rows, :] = x[rows] + _dot_f32(merged, wo_ref[...])
        yield

    tril = tril_ref[...]

    def chunk_rows(ci):
        return slice(ci * CHUNK, (ci + 1) * CHUNK)

    group_tables = [[] for _ in range(n_group)]
    states = [[st_s[j] for j in range(N_PAIR)]]

    def tables_stages(gi):
        chunks = []
        for ci in range(gi * CHUNK_GROUP, (gi + 1) * CHUNK_GROUP):
            rows = chunk_rows(ci)
            chunks.append((r_s[rows, :], k_s[rows, :], v_s[rows, :], a_s[rows, :], b_s[rows, :],
                           lw_s[rows, :]))
        yield from _wkv_tables_stages(chunks, tril, group_tables[gi])

    def apply_stages(gi):
        for i in range(CHUNK_GROUP):
            rows = chunk_rows(gi * CHUNK_GROUP + i)
            tables, e_tot = group_tables[gi][i]
            res = []
            yield from _wkv_apply_stages(tables, v_s[rows, :], e_tot, states[0], res)
            y_s[rows, :] = res[0][0]
            states[0] = res[0][1]

    _interleave(tables_stages(0), branch_a_out_stages())
    for gi in range(n_group):
        stages = [apply_stages(gi), gate_b_stages(gi)]
        if gi + 1 < n_group:
            stages.insert(0, tables_stages(gi + 1))
        if gi > 0:
            stages.append(post_stages(gi - 1))
        _interleave(*stages)
    for j in range(N_PAIR):
        st_s[j] = states[0][j]
    _interleave(post_stages(n_group - 1))


def _mixer(x, n_batch, gain, win, convw, wouta, mu, w0, wdec, a0, wicl, wgate, kk, ka, rk,
           lnw, lnb, woutb, wo):
    m = x.shape[0]
    tiles = m // n_batch // TM_MIX
    head_of = jnp.arange(D_RWKV // 2, dtype=jnp.int32) // HEAD
    bd = (head_of[:, None] == head_of[None, :]).astype(BF)
    idx = jnp.arange(CHUNK, dtype=jnp.int32)
    tril = (idx[None, :] <= idx[:, None]).astype(BF)
    tril = jnp.concatenate([tril, tril, tril], axis=1)
    consts = [gain, win, convw, wouta, mu, w0, wdec, a0, wicl, wgate, kk, ka, rk, lnw, lnb,
              woutb, wo, bd, tril]
    tok_spec = pl.BlockSpec((TM_MIX, D_MODEL), lambda bi, ti: (bi * tiles + ti, 0))
    vec = lambda n: pltpu.VMEM((TM_MIX, n), F32)
    return pl.pallas_call(
        _mixer_body,
        grid=(n_batch, tiles),
        in_specs=[tok_spec] + [_const_spec(c.shape) for c in consts],
        out_specs=tok_spec,
        out_shape=jax.ShapeDtypeStruct((m, D_MODEL), F32),
        scratch_shapes=[
            pltpu.VMEM((TM_MIX + HALO, D_CONV), F32),
            pltpu.VMEM((TM_MIX + HALO, COLS_B), F32),
            vec(D_RWKV), vec(D_RWKV), vec(D_RWKV), vec(D_RWKV), vec(D_RWKV), vec(D_RWKV),
            vec(D_RWKV), vec(D_RWKV), vec(D_RWKV),
            vec(D_MODEL),
            pltpu.VMEM((N_PAIR, PAIR, PAIR), F32),
        ],
        compiler_params=pltpu.CompilerParams(
            dimension_semantics=("arbitrary", "arbitrary"), vmem_limit_bytes=VMEM_LIMIT),
        name="mixer",
    )(x, *consts)


def kernel(x, ffn1_norm, ffn1_w_gate, ffn1_w_up, ffn1_w_down, mix_norm, w_in, conv_w, w_out_a, mu_b, w0, w_decay_up, a0, w_iclr_up, w_gate_up, k_k, k_a, r_k, ln_x_w, ln_x_b, w_out_b, w_o, ffn2_norm, ffn2_w_gate, ffn2_w_up, ffn2_w_down, final_norm):
    n_batch, seq, d = x.shape
    assert d == D_MODEL and seq % TM_MIX == 0 and (n_batch * seq) % TM_FFN == 0
    assert ffn1_norm.shape[0] == 1, "single layer"
    row = lambda t: t.reshape(1, -1).astype(F32)
    xf = x.reshape(n_batch * seq, d)

    x1 = _ffn(xf, row(ffn1_norm[0]), ffn1_w_gate[0], ffn1_w_up[0], ffn1_w_down[0])

    zeros_lora = jnp.zeros((64, D_RWKV), F32)
    wdec = jnp.concatenate([w_decay_up[0], zeros_lora], axis=0)
    wicl = jnp.concatenate([zeros_lora, w_iclr_up[0]], axis=0)
    x2 = _mixer(x1, n_batch, row(mix_norm[0]), w_in[0].astype(BF), conv_w[0].astype(F32),
                w_out_a[0], row(mu_b[0]), row(w0[0]), wdec, row(a0[0]), wicl,
                w_gate_up[0], row(k_k[0]), row(k_a[0]), row(r_k[0]), row(ln_x_w[0]),
                row(ln_x_b[0]), w_out_b[0], w_o[0])

    out = _ffn(x2, row(ffn2_norm[0]), ffn2_w_gate[0], ffn2_w_up[0], ffn2_w_down[0],
               final_gain=row(final_norm))
    return out.reshape(n_batch, seq, d)
```

```python
import functools

import jax
import jax.numpy as jnp
from jax import lax
from jax.experimental import pallas as pl
from jax.experimental.pallas import tpu as pltpu

F32 = jnp.float32
BF = jnp.bfloat16

D_MODEL = 1024
D_CONV = 512
D_RWKV = 512
HEAD = 64
D_FF = 2816
COLS_A = 3 * D_CONV
COLS_B = 3 * D_RWKV + 64 + 64 + 128
OFF_B = COLS_A
OFF_GA = COLS_A + COLS_B
OFF_GB = OFF_GA + D_MODEL
RMS_EPS = 1e-6
GN_EPS = 64e-5
(VEC_MU, VEC_GAIN, VEC_W0, VEC_A0, VEC_KK, VEC_KA, VEC_RK, VEC_LNW, VEC_LNB, VEC_CONV) = range(10)
VEC_ROWS = 16
LORA_ROWS = 128
DECAY_SCALE = 0.6065306597126334

CHUNK = 64
PAIR = 2 * HEAD
N_PAIR = D_RWKV // PAIR
HALO = 8
TM_FFN = 512
TF_FFN = 256
TM_MIX = 512
CHUNK_GROUP = 4
GATE_COLS = 256
VMEM_LIMIT = 56 * 1024 * 1024


def _dot(a, b):
    return jnp.dot(a.astype(BF), b.astype(BF), preferred_element_type=F32)


def _dot_nt(a, b):
    return lax.dot_general(a.astype(BF), b.astype(BF), (((1,), (1,)), ((), ())),
                           preferred_element_type=F32)


def _dot_f32(a, b):
    return jnp.dot(a, b, preferred_element_type=F32)


def _split2(x):
    hi = x.astype(BF)
    lo = (x - hi.astype(F32)).astype(BF)
    return hi, lo


def _dot3(a, b):
    ah, al = _split2(a)
    bh, bl = _split2(b)
    return (jnp.dot(jnp.concatenate([ah, al], axis=1), jnp.concatenate([bh, bh], axis=0),
                    preferred_element_type=F32)
            + jnp.dot(ah, bl, preferred_element_type=F32))


def _dot_exact_rhs(a, b_bf):
    ah, al = _split2(a)
    return (jnp.dot(ah, b_bf, preferred_element_type=F32)
            + jnp.dot(al, b_bf, preferred_element_type=F32))


def _rms_norm(x, gain):
    return x * lax.rsqrt(jnp.mean(x * x, axis=-1, keepdims=True) + RMS_EPS) * gain


def _ffn_body(*refs, final_norm):
    if final_norm:
        x_ref, gain_ref, wg_ref, wu_ref, wd_ref, fn_ref, o_ref, act_ref = refs
    else:
        x_ref, gain_ref, wg_ref, wu_ref, wd_ref, o_ref, act_ref = refs
    x = x_ref[...]
    h = _rms_norm(x, gain_ref[...])
    for c in range(D_FF // TF_FFN):
        sl = slice(c * TF_FFN, (c + 1) * TF_FFN)
        g = _dot_f32(h, wg_ref[:, sl])
        u = _dot_f32(h, wu_ref[:, sl])
        act_ref[:, sl] = g * jax.nn.sigmoid(g) * u
    y = x + 0.5 * _dot_f32(act_ref[...], wd_ref[...])
    if final_norm:
        y = _rms_norm(y, fn_ref[...])
    o_ref[...] = y


def _const_spec(shape):
    return pl.BlockSpec(shape, lambda *_: (0,) * len(shape), pipeline_mode=pl.Buffered(1))


def _ffn(x, gain, wg, wu, wd, final_gain=None):
    m = x.shape[0]
    final_norm = final_gain is not None
    in_specs = [
        pl.BlockSpec((TM_FFN, D_MODEL), lambda i: (i, 0)),
        _const_spec((1, D_MODEL)),
        _const_spec((D_MODEL, D_FF)),
        _const_spec((D_MODEL, D_FF)),
        _const_spec((D_FF, D_MODEL)),
    ]
    args = [x, gain, wg, wu, wd]
    if final_norm:
        in_specs.append(_const_spec((1, D_MODEL)))
        args.append(final_gain)
    return pl.pallas_call(
        functools.partial(_ffn_body, final_norm=final_norm),
        grid=(m // TM_FFN,),
        in_specs=in_specs,
        out_specs=pl.BlockSpec((TM_FFN, D_MODEL), lambda i: (i, 0)),
        out_shape=jax.ShapeDtypeStruct((m, D_MODEL), F32),
        scratch_shapes=[pltpu.VMEM((TM_FFN, D_FF), F32)],
        compiler_params=pltpu.CompilerParams(
            dimension_semantics=("arbitrary",), vmem_limit_bytes=VMEM_LIMIT),
        name="ffn_final" if final_norm else "ffn",
    )(*args)


def _pair_index():
    row = lax.broadcasted_iota(jnp.int32, (PAIR, PAIR), 0)
    lane = lax.broadcasted_iota(jnp.int32, (PAIR, PAIR), 1)
    return row, lane


def _interleave(*gens):
    live = list(gens)
    while live:
        for g in list(live):
            try:
                next(g)
            except StopIteration:
                live.remove(g)


def _tri_inverse_stages(l_bds, out):
    row, lane = _pair_index()
    zero = jnp.zeros((PAIR, PAIR), F32)
    eye = jnp.where(row == lane, 1.0, 0.0).astype(F32)

    def off(m):
        return ((row // (2 * m)) == (lane // (2 * m))) & ((row % (2 * m)) >= m) & ((lane % (2 * m)) < m)

    ts = [eye + jnp.where(off(1), l, zero) for l in l_bds]
    m = 2
    while m < CHUNK:
        mask = off(m)
        tl = [_dot(t, jnp.where(mask, l, zero)) for t, l in zip(ts, l_bds)]
        yield
        ts = [t + _dot(x, t) for t, x in zip(ts, tl)]
        yield
        m *= 2
    out.extend(ts)


def _wkv_tables_stages(chunks, tril, out):
    c = CHUNK
    head0 = lax.broadcasted_iota(jnp.int32, (c, PAIR), 1) < HEAD
    row, lane = _pair_index()
    t_idx = row % c
    s_idx = lane % c
    blockdiag = (row // c) == (lane // c)
    zero = jnp.zeros((c, PAIR), F32)
    zero2 = jnp.zeros((PAIR, PAIR), F32)

    e_tots, lhs0, lhs1, rhs0, rhs1, at_bd, rt_bd, bkh, v_swap = [], [], [], [], [], [], [], [], []
    for r, k, v, a, b, lw in chunks:
        h1 = lw.astype(BF)
        r1 = lw - h1.astype(F32)
        h2 = r1.astype(BF)
        h3 = (r1 - h2.astype(F32)).astype(BF)
        cs = jnp.dot(tril, jnp.concatenate([h1, h2, h3], axis=0), preferred_element_type=F32)
        tot = cs[c - 1:c, :]
        e_tots.append(jnp.exp(tot))
        at = a * jnp.exp(cs - lw)
        rt = r * jnp.exp(cs)
        e_n = jnp.exp(-cs)
        bt = b * e_n
        kt = k * e_n
        bh = bt * e_tots[-1]
        kh = kt * e_tots[-1]
        for j in range(N_PAIR):
            sl = slice(j * PAIR, (j + 1) * PAIR)
            ar0 = jnp.concatenate([jnp.where(head0, at[:, sl], zero), jnp.where(head0, rt[:, sl], zero)], axis=0)
            ar1 = jnp.concatenate([jnp.where(head0, zero, at[:, sl]), jnp.where(head0, zero, rt[:, sl])], axis=0)
            lhs0.append(ar0)
            lhs1.append(ar1)
            rhs0.append(jnp.concatenate([bt[:, sl], kt[:, sl]], axis=0))
            rhs1.append(jnp.concatenate([kt[:, sl], bt[:, sl]], axis=0))
            at_bd.append(jnp.concatenate([ar0[:c], ar1[:c]], axis=0))
            rt_bd.append(jnp.concatenate([ar0[c:], ar1[c:]], axis=0))
            bkh.append(jnp.concatenate([bh[:, sl], kh[:, sl]], axis=0))
            v_swap.append(jnp.concatenate([jnp.where(head0, zero, v[:, sl]),
                                           jnp.where(head0, v[:, sl], zero)], axis=0))
    yield
    g0 = [_dot_nt(x, y) for x, y in zip(lhs0, rhs0)]
    g1 = [_dot_nt(x, y) for x, y in zip(lhs1, rhs1)]
    yield
    aa = [jnp.where(s_idx < t_idx, jnp.concatenate([x[:c], y[:c]], axis=0), zero2) for x, y in zip(g0, g1)]
    arr = [jnp.where(s_idx <= t_idx, jnp.concatenate([x[c:], y[c:]], axis=0), zero2) for x, y in zip(g0, g1)]
    l_bd = [jnp.where(blockdiag, x, zero2) for x in aa]
    ak_ad = [jnp.where(blockdiag, zero2, x) for x in aa]
    akv = [_dot(x, y) for x, y in zip(ak_ad, v_swap)]
    yield
    t_inv = []
    yield from _tri_inverse_stages(l_bd, t_inv)
    t_at_akv = [_dot(t, jnp.concatenate([x, w], axis=1)) for t, x, w in zip(t_inv, at_bd, akv)]
    yield
    tar = [jnp.concatenate([z[:, :PAIR], y], axis=0) for z, y in zip(t_at_akv, rt_bd)]
    u_v = [z[:, PAIR:] for z in t_at_akv]
    for i in range(len(chunks)):
        sl = slice(i * N_PAIR, (i + 1) * N_PAIR)
        out.append(((tar[sl], u_v[sl], arr[sl], bkh[sl], v_swap[sl]), e_tots[i]))


def _wkv_apply_stages(tables, v, e_tot, states, out):
    c = CHUNK
    tar, u_v, arr, bkh, v_swap = tables
    head0 = lax.broadcasted_iota(jnp.int32, (c, PAIR), 1) < HEAD
    row, lane = _pair_index()
    blockdiag = (row // c) == (lane // c)
    zero2 = jnp.zeros((PAIR, PAIR), F32)
    xs = [_dot_nt(x, s) for x, s in zip(tar, states)]
    yield
    u_bd = [z[:PAIR] + w for z, w in zip(xs, u_v)]
    rs_bd = [z[PAIR:] for z in xs]
    y_bd = [_dot(m, u + w) + z for m, u, w, z in zip(arr, u_bd, v_swap, rs_bd)]
    ys = [jnp.where(head0, z[:c], z[c:]) for z in y_bd]
    new_states = []
    for j in range(N_PAIR):
        sl = slice(j * PAIR, (j + 1) * PAIR)
        uv = jnp.concatenate([u_bd[j][:c] + u_bd[j][c:], v[:, sl]], axis=0)
        upd = _dot(uv.T, bkh[j])
        new_states.append(jnp.where(blockdiag, states[j] * e_tot[:, sl] + upd, zero2))
    yield
    out.append((jnp.concatenate(ys, axis=1), new_states))


def _mixer_body(x_ref, vec_ref, win_ref, wouta_ref, lora_ref, woutb_ref, wo_ref,
                o_ref,
                cu_buf, pb_buf, r_s, k_s, v_s, a_s, b_s, lw_s, g_s, bon_s, y_s, m_s, st_s):
    tm = TM_MIX

    @pl.when(pl.program_id(1) == 0)
    def _():
        cu_buf[0:HALO, :] = jnp.zeros((HALO, D_CONV), F32)
        pb_buf[0:HALO, :] = jnp.zeros((HALO, COLS_B), F32)
        st_s[...] = jnp.zeros_like(st_s)

    def vec(row, n, rows=1):
        return vec_ref[row:row + rows, 0:n]

    x = x_ref[...]
    h = _rms_norm(x, vec(VEC_GAIN, D_MODEL)).astype(BF)
    bd_r = lax.broadcasted_iota(jnp.int32, (D_RWKV // 2, D_RWKV // 2), 0) // HEAD
    bd_c = lax.broadcasted_iota(jnp.int32, (D_RWKV // 2, D_RWKV // 2), 1) // HEAD
    bd = jnp.where(bd_r == bd_c, 1.0, 0.0).astype(BF)

    def head_sum(z):
        half = D_RWKV // 2
        n = z.shape[0]
        s = _dot_exact_rhs(jnp.concatenate([z[:, :half], z[:, half:]], axis=0), bd)
        return jnp.concatenate([s[:n], s[n:]], axis=1)

    def proj(lo, hi):
        return jnp.dot(h, win_ref[:, lo:hi], preferred_element_type=F32)

    pa_parts = []

    def branch_a_proj_stages():
        for p in range(COLS_A // D_CONV):
            pa_parts.append(proj(p * D_CONV, (p + 1) * D_CONV))
            yield

    def branch_b_prep_stages():
        pb = proj(OFF_B, OFF_GA)
        yield
        pb_buf[HALO:HALO + tm, :] = pb
        prev = pb_buf[pl.ds(HALO - 1, tm), :]
        pb_buf[0:HALO, :] = pb_buf[tm:tm + HALO, :]
        pbm = pb + (prev - pb) * vec(VEC_MU, COLS_B)
        r = pbm[:, 0:D_RWKV]
        k = pbm[:, D_RWKV:2 * D_RWKV]
        v = pbm[:, 2 * D_RWKV:3 * D_RWKV]
        xwa = pbm[:, 3 * D_RWKV:3 * D_RWKV + 128]
        xg = pbm[:, 3 * D_RWKV + 128:COLS_B]
        z = vec(VEC_W0, D_RWKV) + _dot3(jnp.tanh(xwa), lora_ref[0:LORA_ROWS, :])
        lw_s[...] = -DECAY_SCALE * jax.nn.sigmoid(z)
        iclr = jax.nn.sigmoid(vec(VEC_A0, D_RWKV) + _dot3(xwa, lora_ref[LORA_ROWS:2 * LORA_ROWS, :]))
        g_s[...] = _dot_f32(jax.nn.sigmoid(xg), lora_ref[2 * LORA_ROWS:3 * LORA_ROWS, :])
        yield
        kk = k * vec(VEC_KK, D_RWKV)
        kk = kk * lax.rsqrt(jnp.maximum(head_sum(kk * kk), 1e-24))
        k2 = k * (1.0 + (iclr - 1.0) * vec(VEC_KA, D_RWKV))
        r_s[...] = r
        k_s[...] = k2
        v_s[...] = v
        a_s[...] = -kk
        b_s[...] = kk * iclr
        yield
        bon_s[...] = head_sum(r * k2 * vec(VEC_RK, D_RWKV)) * v
        yield

    _interleave(branch_b_prep_stages(), branch_a_proj_stages())

    cu = pa_parts[1] * pa_parts[2]
    cu_buf[HALO:HALO + tm, :] = cu
    cw = vec(VEC_CONV, D_CONV, rows=3)
    conv = (cw[2:3] * cu + cw[1:2] * cu_buf[pl.ds(HALO - 1, tm), :]
            + cw[0:1] * cu_buf[pl.ds(HALO - 2, tm), :])
    cu_buf[0:HALO, :] = cu_buf[tm:tm + HALO, :]
    gated = pa_parts[0] * conv

    def branch_a_out_stages():
        for q in range(D_MODEL // GATE_COLS):
            cols = slice(q * GATE_COLS, (q + 1) * GATE_COLS)
            ya = _dot_f32(gated, wouta_ref[:, cols])
            ga = proj(OFF_GA + q * GATE_COLS, OFF_GA + (q + 1) * GATE_COLS)
            m_s[:, cols] = jax.nn.sigmoid(ga) * ya
            yield

    group_rows = CHUNK_GROUP * CHUNK
    n_group = tm // group_rows
    sgb_parts = [[] for _ in range(n_group)]

    def gate_b_stages(gi):
        rows = slice(gi * group_rows, (gi + 1) * group_rows)
        for q in range(D_MODEL // GATE_COLS):
            lo = OFF_GB + q * GATE_COLS
            gb = jnp.dot(h[rows], win_ref[:, lo:lo + GATE_COLS], preferred_element_type=F32)
            sgb_parts[gi].append(jax.nn.sigmoid(gb))
            yield

    def post_stages(gi):
        rows = slice(gi * group_rows, (gi + 1) * group_rows)
        y = y_s[rows, :]
        mean = head_sum(y) * (1.0 / HEAD)
        yield
        d = y - mean
        var = head_sum(d * d) * (1.0 / HEAD)
        yield
        yn = (d * lax.rsqrt(var + GN_EPS) * vec(VEC_LNW, D_RWKV) + vec(VEC_LNB, D_RWKV)
              + bon_s[rows, :])
        yb = _dot_f32(yn * g_s[rows, :], woutb_ref[...])
        yield
        merged = m_s[rows, :] + jnp.concatenate(sgb_parts[gi], axis=1) * yb
        o_ref[rows, :] = x[rows] + _dot_f32(merged, wo_ref[...])
        yield

    tril = jnp.where(lax.broadcasted_iota(jnp.int32, (CHUNK, 3 * CHUNK), 1) % CHUNK
                     <= lax.broadcasted_iota(jnp.int32, (CHUNK, 3 * CHUNK), 0), 1.0, 0.0).astype(BF)

    def chunk_rows(ci):
        return slice(ci * CHUNK, (ci + 1) * CHUNK)

    group_tables = [[] for _ in range(n_group)]
    states = [[st_s[j] for j in range(N_PAIR)]]

    def tables_stages(gi):
        chunks = []
        for ci in range(gi * CHUNK_GROUP, (gi + 1) * CHUNK_GROUP):
            rows = chunk_rows(ci)
            chunks.append((r_s[rows, :], k_s[rows, :], v_s[rows, :], a_s[rows, :], b_s[rows, :],
                           lw_s[rows, :]))
        yield from _wkv_tables_stages(chunks, tril, group_tables[gi])

    def apply_stages(gi):
        for i in range(CHUNK_GROUP):
            rows = chunk_rows(gi * CHUNK_GROUP + i)
            tables, e_tot = group_tables[gi][i]
            res = []
            yield from _wkv_apply_stages(tables, v_s[rows, :], e_tot, states[0], res)
            y_s[rows, :] = res[0][0]
            states[0] = res[0][1]

    _interleave(tables_stages(0), branch_a_out_stages())
    for gi in range(n_group):
        stages = [apply_stages(gi), gate_b_stages(gi)]
        if gi + 1 < n_group:
            stages.insert(0, tables_stages(gi + 1))
        if gi > 0:
            stages.append(post_stages(gi - 1))
        _interleave(*stages)
    for j in range(N_PAIR):
        st_s[j] = states[0][j]
    _interleave(post_stages(n_group - 1))


def _mixer(x, n_batch, vecs, win, wouta, lora, woutb, wo):
    m = x.shape[0]
    tiles = m // n_batch // TM_MIX
    consts = [vecs, win, wouta, lora, woutb, wo]
    tok_spec = pl.BlockSpec((TM_MIX, D_MODEL), lambda bi, ti: (bi * tiles + ti, 0))
    vec = lambda n: pltpu.VMEM((TM_MIX, n), F32)
    return pl.pallas_call(
        _mixer_body,
        grid=(n_batch, tiles),
        in_specs=[tok_spec] + [_const_spec(c.shape) for c in consts],
        out_specs=tok_spec,
        out_shape=jax.ShapeDtypeStruct((m, D_MODEL), F32),
        scratch_shapes=[
            pltpu.VMEM((TM_MIX + HALO, D_CONV), F32),
            pltpu.VMEM((TM_MIX + HALO, COLS_B), F32),
            vec(D_RWKV), vec(D_RWKV), vec(D_RWKV), vec(D_RWKV), vec(D_RWKV), vec(D_RWKV),
            vec(D_RWKV), vec(D_RWKV), vec(D_RWKV),
            vec(D_MODEL),
            pltpu.VMEM((N_PAIR, PAIR, PAIR), F32),
        ],
        compiler_params=pltpu.CompilerParams(
            dimension_semantics=("arbitrary", "arbitrary"), vmem_limit_bytes=VMEM_LIMIT),
        name="mixer",
    )(x, *consts)


def kernel(x, ffn1_norm, ffn1_w_gate, ffn1_w_up, ffn1_w_down, mix_norm, w_in, conv_w, w_out_a, mu_b, w0, w_decay_up, a0, w_iclr_up, w_gate_up, k_k, k_a, r_k, ln_x_w, ln_x_b, w_out_b, w_o, ffn2_norm, ffn2_w_gate, ffn2_w_up, ffn2_w_down, final_norm):
    n_batch, seq, d = x.shape
    assert d == D_MODEL and seq % TM_MIX == 0 and (n_batch * seq) % TM_FFN == 0
    assert ffn1_norm.shape[0] == 1, "single layer"
    row = lambda t: t.reshape(1, -1).astype(F32)
    xf = x.reshape(n_batch * seq, d)

    x1 = _ffn(xf, row(ffn1_norm[0]), ffn1_w_gate[0], ffn1_w_up[0], ffn1_w_down[0])

    def vec_rows(t, n_rows=1):
        t = t.reshape(n_rows, -1).astype(F32)
        return jnp.pad(t, ((0, 0), (0, COLS_B - t.shape[-1])))

    rows = [mu_b[0], mix_norm[0], w0[0], a0[0], k_k[0], k_a[0], r_k[0], ln_x_w[0], ln_x_b[0]]
    vecs = jnp.concatenate([vec_rows(t) for t in rows] + [vec_rows(conv_w[0], 3)], axis=0)
    vecs = jnp.pad(vecs, ((0, VEC_ROWS - vecs.shape[0]), (0, 0)))
    zeros_lora = jnp.zeros((64, D_RWKV), F32)
    lora = jnp.concatenate([w_decay_up[0], zeros_lora, zeros_lora, w_iclr_up[0], w_gate_up[0]], axis=0)
    x2 = _mixer(x1, n_batch, vecs, w_in[0].astype(BF), w_out_a[0], lora, w_out_b[0], w_o[0])

    out = _ffn(x2, row(ffn2_norm[0]), ffn2_w_gate[0], ffn2_w_up[0], ffn2_w_down[0],
               final_gain=row(final_norm))
    return out.reshape(n_batch, seq, d)
```

```python
import functools

import jax
import jax.numpy as jnp
from jax import lax
from jax.experimental import pallas as pl
from jax.experimental.pallas import tpu as pltpu

F32 = jnp.float32
BF = jnp.bfloat16

D_MODEL = 1024
D_CONV = 512
D_RWKV = 512
HEAD = 64
D_FF = 2816
COLS_A = 3 * D_CONV
COLS_B = 3 * D_RWKV + 64 + 64 + 128
OFF_B = COLS_A
OFF_GA = COLS_A + COLS_B
OFF_GB = OFF_GA + D_MODEL
RMS_EPS = 1e-6
GN_EPS = 64e-5
(VEC_MU, VEC_GAIN, VEC_W0, VEC_A0, VEC_KK, VEC_KA, VEC_RK, VEC_LNW, VEC_LNB, VEC_CONV) = range(10)
VEC_ROWS = 16
LORA_ROWS = 128
DECAY_SCALE = 0.6065306597126334

CHUNK = 64
PAIR = 2 * HEAD
N_PAIR = D_RWKV // PAIR
HALO = 8
TM_FFN = 512
TF_FFN = 256
TM_MIX = 512
CHUNK_GROUP = 4
GATE_COLS = 256
VMEM_LIMIT = 56 * 1024 * 1024


def _dot(a, b):
    return jnp.dot(a.astype(BF), b.astype(BF), preferred_element_type=F32)


def _dot_nt(a, b):
    return lax.dot_general(a.astype(BF), b.astype(BF), (((1,), (1,)), ((), ())),
                           preferred_element_type=F32)


def _dot_f32(a, b):
    return jnp.dot(a, b, preferred_element_type=F32)


def _split2(x):
    hi = x.astype(BF)
    lo = (x - hi.astype(F32)).astype(BF)
    return hi, lo


def _dot3(a, b):
    ah, al = _split2(a)
    bh, bl = _split2(b)
    return (jnp.dot(jnp.concatenate([ah, al], axis=1), jnp.concatenate([bh, bh], axis=0),
                    preferred_element_type=F32)
            + jnp.dot(ah, bl, preferred_element_type=F32))


def _dot_exact_rhs(a, b_bf):
    ah, al = _split2(a)
    return (jnp.dot(ah, b_bf, preferred_element_type=F32)
            + jnp.dot(al, b_bf, preferred_element_type=F32))


def _rms_norm(x, gain):
    return x * lax.rsqrt(jnp.mean(x * x, axis=-1, keepdims=True) + RMS_EPS) * gain


def _ffn_body(*refs, final_norm):
    if final_norm:
        x_ref, gain_ref, wg_ref, wu_ref, wd_ref, fn_ref, o_ref, act_ref = refs
    else:
        x_ref, gain_ref, wg_ref, wu_ref, wd_ref, o_ref, act_ref = refs
    x = x_ref[...]
    h = _rms_norm(x, gain_ref[...])
    for c in range(D_FF // TF_FFN):
        sl = slice(c * TF_FFN, (c + 1) * TF_FFN)
        g = _dot_f32(h, wg_ref[:, sl])
        u = _dot_f32(h, wu_ref[:, sl])
        act_ref[:, sl] = g * jax.nn.sigmoid(g) * u
    y = x + 0.5 * _dot_f32(act_ref[...], wd_ref[...])
    if final_norm:
        y = _rms_norm(y, fn_ref[...])
    o_ref[...] = y


def _const_spec(shape):
    return pl.BlockSpec(shape, lambda *_: (0,) * len(shape), pipeline_mode=pl.Buffered(1))


def _ffn(x, gain, wg, wu, wd, final_gain=None):
    m = x.shape[0]
    final_norm = final_gain is not None
    in_specs = [
        pl.BlockSpec((TM_FFN, D_MODEL), lambda i: (i, 0)),
        _const_spec((1, D_MODEL)),
        _const_spec((D_MODEL, D_FF)),
        _const_spec((D_MODEL, D_FF)),
        _const_spec((D_FF, D_MODEL)),
    ]
    args = [x, gain, wg, wu, wd]
    if final_norm:
        in_specs.append(_const_spec((1, D_MODEL)))
        args.append(final_gain)
    return pl.pallas_call(
        functools.partial(_ffn_body, final_norm=final_norm),
        grid=(m // TM_FFN,),
        in_specs=in_specs,
        out_specs=pl.BlockSpec((TM_FFN, D_MODEL), lambda i: (i, 0)),
        out_shape=jax.ShapeDtypeStruct((m, D_MODEL), F32),
        scratch_shapes=[pltpu.VMEM((TM_FFN, D_FF), F32)],
        compiler_params=pltpu.CompilerParams(
            dimension_semantics=("arbitrary",), vmem_limit_bytes=VMEM_LIMIT),
        name="ffn_final" if final_norm else "ffn",
    )(*args)


def _pair_index():
    row = lax.broadcasted_iota(jnp.int32, (PAIR, PAIR), 0)
    lane = lax.broadcasted_iota(jnp.int32, (PAIR, PAIR), 1)
    return row, lane


def _interleave(*gens):
    live = list(gens)
    while live:
        for g in list(live):
            try:
                next(g)
            except StopIteration:
                live.remove(g)


def _tri_inverse_stages(l_bds, out):
    row, lane = _pair_index()
    zero = jnp.zeros((PAIR, PAIR), F32)
    eye = jnp.where(row == lane, 1.0, 0.0).astype(F32)

    def off(m):
        return ((row // (2 * m)) == (lane // (2 * m))) & ((row % (2 * m)) >= m) & ((lane % (2 * m)) < m)

    ts = [eye + jnp.where(off(1), l, zero) for l in l_bds]
    m = 2
    while m < CHUNK:
        mask = off(m)
        tl = [_dot(t, jnp.where(mask, l, zero)) for t, l in zip(ts, l_bds)]
        yield
        ts = [t + _dot(x, t) for t, x in zip(ts, tl)]
        yield
        m *= 2
    out.extend(ts)


def _wkv_tables_stages(chunks, tril, out):
    c = CHUNK
    head0 = lax.broadcasted_iota(jnp.int32, (c, PAIR), 1) < HEAD
    row, lane = _pair_index()
    t_idx = row % c
    s_idx = lane % c
    blockdiag = (row // c) == (lane // c)
    zero = jnp.zeros((c, PAIR), F32)
    zero2 = jnp.zeros((PAIR, PAIR), F32)

    e_tots, lhs0, lhs1, rhs0, rhs1, at_bd, rt_bd, bkh, v_swap = [], [], [], [], [], [], [], [], []
    for r, k, v, a, b, lw in chunks:
        h1 = lw.astype(BF)
        r1 = lw - h1.astype(F32)
        h2 = r1.astype(BF)
        h3 = (r1 - h2.astype(F32)).astype(BF)
        cs = jnp.dot(tril, jnp.concatenate([h1, h2, h3], axis=0), preferred_element_type=F32)
        tot = cs[c - 1:c, :]
        e_tots.append(jnp.exp(tot))
        at = a * jnp.exp(cs - lw)
        rt = r * jnp.exp(cs)
        e_n = jnp.exp(-cs)
        bt = b * e_n
        kt = k * e_n
        bh = bt * e_tots[-1]
        kh = kt * e_tots[-1]
        for j in range(N_PAIR):
            sl = slice(j * PAIR, (j + 1) * PAIR)
            ar0 = jnp.concatenate([jnp.where(head0, at[:, sl], zero), jnp.where(head0, rt[:, sl], zero)], axis=0)
            ar1 = jnp.concatenate([jnp.where(head0, zero, at[:, sl]), jnp.where(head0, zero, rt[:, sl])], axis=0)
            lhs0.append(ar0)
            lhs1.append(ar1)
            rhs0.append(jnp.concatenate([bt[:, sl], kt[:, sl]], axis=0))
            rhs1.append(jnp.concatenate([kt[:, sl], bt[:, sl]], axis=0))
            at_bd.append(jnp.concatenate([ar0[:c], ar1[:c]], axis=0))
            rt_bd.append(jnp.concatenate([ar0[c:], ar1[c:]], axis=0))
            bkh.append(jnp.concatenate([bh[:, sl], kh[:, sl]], axis=0))
            v_swap.append(jnp.concatenate([jnp.where(head0, zero, v[:, sl]),
                                           jnp.where(head0, v[:, sl], zero)], axis=0))
    yield
    g0 = [_dot_nt(x, y) for x, y in zip(lhs0, rhs0)]
    g1 = [_dot_nt(x, y) for x, y in zip(lhs1, rhs1)]
    yield
    aa = [jnp.where(s_idx < t_idx, jnp.concatenate([x[:c], y[:c]], axis=0), zero2) for x, y in zip(g0, g1)]
    arr = [jnp.where(s_idx <= t_idx, jnp.concatenate([x[c:], y[c:]], axis=0), zero2) for x, y in zip(g0, g1)]
    l_bd = [jnp.where(blockdiag, x, zero2) for x in aa]
    ak_ad = [jnp.where(blockdiag, zero2, x) for x in aa]
    akv = [_dot(x, y) for x, y in zip(ak_ad, v_swap)]
    yield
    t_inv = []
    yield from _tri_inverse_stages(l_bd, t_inv)
    t_at_akv = [_dot(t, jnp.concatenate([x, w], axis=1)) for t, x, w in zip(t_inv, at_bd, akv)]
    yield
    tar = [jnp.concatenate([z[:, :PAIR], y], axis=0) for z, y in zip(t_at_akv, rt_bd)]
    u_v = [z[:, PAIR:] for z in t_at_akv]
    for i in range(len(chunks)):
        sl = slice(i * N_PAIR, (i + 1) * N_PAIR)
        out.append(((tar[sl], u_v[sl], arr[sl], bkh[sl], v_swap[sl]), e_tots[i]))


def _wkv_apply_stages(tables, v, e_tot, states, out):
    c = CHUNK
    tar, u_v, arr, bkh, v_swap = tables
    head0 = lax.broadcasted_iota(jnp.int32, (c, PAIR), 1) < HEAD
    row, lane = _pair_index()
    blockdiag = (row // c) == (lane // c)
    zero2 = jnp.zeros((PAIR, PAIR), F32)
    xs = [_dot_nt(x, s) for x, s in zip(tar, states)]
    yield
    u_bd = [z[:PAIR] + w for z, w in zip(xs, u_v)]
    rs_bd = [z[PAIR:] for z in xs]
    y_bd = [_dot(m, u + w) + z for m, u, w, z in zip(arr, u_bd, v_swap, rs_bd)]
    ys = [jnp.where(head0, z[:c], z[c:]) for z in y_bd]
    new_states = []
    for j in range(N_PAIR):
        sl = slice(j * PAIR, (j + 1) * PAIR)
        uv = jnp.concatenate([u_bd[j][:c] + u_bd[j][c:], v[:, sl]], axis=0)
        upd = _dot(uv.T, bkh[j])
        new_states.append(jnp.where(blockdiag, states[j] * e_tot[:, sl] + upd, zero2))
    yield
    out.append((jnp.concatenate(ys, axis=1), new_states))


def _mixer_body(x_ref, vec_ref, win_ref, wouta_ref, lora_ref, woutb_ref, wo_ref,
                o_ref,
                cu_buf, pb_buf, r_s, k_s, v_s, a_s, b_s, lw_s, g_s, bon_s, y_s, m_s, st_s):
    tm = TM_MIX

    @pl.when(pl.program_id(1) == 0)
    def _():
        cu_buf[0:HALO, :] = jnp.zeros((HALO, D_CONV), F32)
        pb_buf[0:HALO, :] = jnp.zeros((HALO, COLS_B), F32)
        st_s[...] = jnp.zeros_like(st_s)

    def vec(row, n, rows=1):
        return vec_ref[row:row + rows, 0:n]

    x = x_ref[...]
    h = _rms_norm(x, vec(VEC_GAIN, D_MODEL)).astype(BF)
    bd_r = lax.broadcasted_iota(jnp.int32, (D_RWKV // 2, D_RWKV // 2), 0) // HEAD
    bd_c = lax.broadcasted_iota(jnp.int32, (D_RWKV // 2, D_RWKV // 2), 1) // HEAD
    bd = jnp.where(bd_r == bd_c, 1.0, 0.0).astype(BF)

    def head_sum(z):
        half = D_RWKV // 2
        n = z.shape[0]
        s = _dot_exact_rhs(jnp.concatenate([z[:, :half], z[:, half:]], axis=0), bd)
        return jnp.concatenate([s[:n], s[n:]], axis=1)

    def proj(lo, hi):
        return jnp.dot(h, win_ref[:, lo:hi], preferred_element_type=F32)

    pa_parts = []

    def branch_a_proj_stages():
        for p in range(COLS_A // D_CONV):
            pa_parts.append(proj(p * D_CONV, (p + 1) * D_CONV))
            yield

    def branch_b_prep_stages():
        pb = proj(OFF_B, OFF_GA)
        yield
        pb_buf[HALO:HALO + tm, :] = pb
        prev = pb_buf[pl.ds(HALO - 1, tm), :]
        pb_buf[0:HALO, :] = pb_buf[tm:tm + HALO, :]
        pbm = pb + (prev - pb) * vec(VEC_MU, COLS_B)
        r = pbm[:, 0:D_RWKV]
        k = pbm[:, D_RWKV:2 * D_RWKV]
        v = pbm[:, 2 * D_RWKV:3 * D_RWKV]
        xwa = pbm[:, 3 * D_RWKV:3 * D_RWKV + 128]
        xg = pbm[:, 3 * D_RWKV + 128:COLS_B]
        z = vec(VEC_W0, D_RWKV) + _dot3(jnp.tanh(xwa), lora_ref[0:LORA_ROWS, :])
        lw_s[...] = -DECAY_SCALE * jax.nn.sigmoid(z)
        iclr = jax.nn.sigmoid(vec(VEC_A0, D_RWKV) + _dot3(xwa, lora_ref[LORA_ROWS:2 * LORA_ROWS, :]))
        g_s[...] = _dot_f32(jax.nn.sigmoid(xg), lora_ref[2 * LORA_ROWS:3 * LORA_ROWS, :])
        yield
        kk = k * vec(VEC_KK, D_RWKV)
        kk = kk * lax.rsqrt(jnp.maximum(head_sum(kk * kk), 1e-24))
        k2 = k * (1.0 + (iclr - 1.0) * vec(VEC_KA, D_RWKV))
        r_s[...] = r
        k_s[...] = k2
        v_s[...] = v
        a_s[...] = -kk
        b_s[...] = kk * iclr
        yield
        bon_s[...] = head_sum(r * k2 * vec(VEC_RK, D_RWKV)) * v
        yield

    _interleave(branch_b_prep_stages(), branch_a_proj_stages())

    cu = pa_parts[1] * pa_parts[2]
    cu_buf[HALO:HALO + tm, :] = cu
    cw = vec(VEC_CONV, D_CONV, rows=3)
    conv = (cw[2:3] * cu + cw[1:2] * cu_buf[pl.ds(HALO - 1, tm), :]
            + cw[0:1] * cu_buf[pl.ds(HALO - 2, tm), :])
    cu_buf[0:HALO, :] = cu_buf[tm:tm + HALO, :]
    gated = pa_parts[0] * conv

    def branch_a_out_stages():
        for q in range(D_MODEL // GATE_COLS):
            cols = slice(q * GATE_COLS, (q + 1) * GATE_COLS)
            ya = _dot_f32(gated, wouta_ref[:, cols])
            ga = proj(OFF_GA + q * GATE_COLS, OFF_GA + (q + 1) * GATE_COLS)
            m_s[:, cols] = jax.nn.sigmoid(ga) * ya
            yield

    group_rows = CHUNK_GROUP * CHUNK
    n_group = tm // group_rows
    sgb_parts = [[] for _ in range(n_group)]

    def gate_b_stages(gi):
        rows = slice(gi * group_rows, (gi + 1) * group_rows)
        for q in range(D_MODEL // GATE_COLS):
            lo = OFF_GB + q * GATE_COLS
            gb = jnp.dot(h[rows], win_ref[:, lo:lo + GATE_COLS], preferred_element_type=F32)
            sgb_parts[gi].append(jax.nn.sigmoid(gb))
            yield

    def post_stages(gi):
        rows = slice(gi * group_rows, (gi + 1) * group_rows)
        y = y_s[rows, :]
        mean = head_sum(y) * (1.0 / HEAD)
        yield
        d = y - mean
        var = head_sum(d * d) * (1.0 / HEAD)
        yield
        yn = (d * lax.rsqrt(var + GN_EPS) * vec(VEC_LNW, D_RWKV) + vec(VEC_LNB, D_RWKV)
              + bon_s[rows, :])
        yb = _dot_f32(yn * g_s[rows, :], woutb_ref[...])
        yield
        merged = m_s[rows, :] + jnp.concatenate(sgb_parts[gi], axis=1) * yb
        o_ref[rows, :] = x[rows] + _dot_f32(merged, wo_ref[...])
        yield

    tril = jnp.where(lax.broadcasted_iota(jnp.int32, (CHUNK, 3 * CHUNK), 1) % CHUNK
                     <= lax.broadcasted_iota(jnp.int32, (CHUNK, 3 * CHUNK), 0), 1.0, 0.0).astype(BF)

    def chunk_rows(ci):
        return slice(ci * CHUNK, (ci + 1) * CHUNK)

    group_tables = [[] for _ in range(n_group)]
    states = [[st_s[j] for j in range(N_PAIR)]]

    def tables_stages(gi):
        chunks = []
        for ci in range(gi * CHUNK_GROUP, (gi + 1) * CHUNK_GROUP):
            rows = chunk_rows(ci)
            chunks.append((r_s[rows, :], k_s[rows, :], v_s[rows, :], a_s[rows, :], b_s[rows, :],
                           lw_s[rows, :]))
        yield from _wkv_tables_stages(chunks, tril, group_tables[gi])

    def apply_stages(gi):
        for i in range(CHUNK_GROUP):
            rows = chunk_rows(gi * CHUNK_GROUP + i)
            tables, e_tot = group_tables[gi][i]
            res = []
            yield from _wkv_apply_stages(tables, v_s[rows, :], e_tot, states[0], res)
            y_s[rows, :] = res[0][0]
            states[0] = res[0][1]

    _interleave(tables_stages(0), branch_a_out_stages())
    for gi in range(n_group):
        stages = [apply_stages(gi), gate_b_stages(gi)]
        if gi + 1 < n_group:
            stages.insert(0, tables_stages(gi + 1))
        if gi > 0:
            stages.append(post_stages(gi - 1))
        _interleave(*stages)
    for j in range(N_PAIR):
        st_s[j] = states[0][j]
    _interleave(post_stages(n_group - 1))


def _mixer(x, n_batch, vecs, win, wouta, lora, woutb, wo):
    m = x.shape[0]
    tiles = m // n_batch // TM_MIX
    consts = [vecs, win, wouta, lora, woutb, wo]
    tok_spec = pl.BlockSpec((TM_MIX, D_MODEL), lambda bi, ti: (bi * tiles + ti, 0))
    vec = lambda n: pltpu.VMEM((TM_MIX, n), F32)
    return pl.pallas_call(
        _mixer_body,
        grid=(n_batch, tiles),
        in_specs=[tok_spec] + [_const_spec(c.shape) for c in consts],
        out_specs=tok_spec,
        out_shape=jax.ShapeDtypeStruct((m, D_MODEL), F32),
        scratch_shapes=[
            pltpu.VMEM((TM_MIX + HALO, D_CONV), F32),
            pltpu.VMEM((TM_MIX + HALO, COLS_B), F32),
            vec(D_RWKV), vec(D_RWKV), vec(D_RWKV), vec(D_RWKV), vec(D_RWKV), vec(D_RWKV),
            vec(D_RWKV), vec(D_RWKV), vec(D_RWKV),
            vec(D_MODEL),
            pltpu.VMEM((N_PAIR, PAIR, PAIR), F32),
        ],
        compiler_params=pltpu.CompilerParams(
            dimension_semantics=("arbitrary", "arbitrary"), vmem_limit_bytes=VMEM_LIMIT),
        name="mixer",
    )(x, *consts)


def kernel(x, ffn1_norm, ffn1_w_gate, ffn1_w_up, ffn1_w_down, mix_norm, w_in, conv_w, w_out_a, mu_b, w0, w_decay_up, a0, w_iclr_up, w_gate_up, k_k, k_a, r_k, ln_x_w, ln_x_b, w_out_b, w_o, ffn2_norm, ffn2_w_gate, ffn2_w_up, ffn2_w_down, final_norm):
    n_batch, seq, d = x.shape
    assert d == D_MODEL and seq % TM_MIX == 0 and (n_batch * seq) % TM_FFN == 0
    assert ffn1_norm.shape[0] == 1, "single layer"
    row = lambda t: t.reshape(1, -1).astype(F32)
    xf = x.reshape(n_batch * seq, d)

    x1 = _ffn(xf, row(ffn1_norm[0]), ffn1_w_gate[0], ffn1_w_up[0], ffn1_w_down[0])

    rows = [mu_b[0], mix_norm[0], w0[0], a0[0], k_k[0], k_a[0], r_k[0], ln_x_w[0], ln_x_b[0],
            conv_w[0][0], conv_w[0][1], conv_w[0][2]]
    pieces = []
    for t in rows:
        t = t.reshape(-1).astype(F32)
        pieces += [t, jnp.zeros((COLS_B - t.size,), F32)]
    pieces.append(jnp.zeros(((VEC_ROWS - len(rows)) * COLS_B,), F32))
    vecs = jnp.concatenate(pieces).reshape(VEC_ROWS, COLS_B)
    zeros_lora = jnp.zeros((64, D_RWKV), F32)
    lora = jnp.concatenate([w_decay_up[0], zeros_lora, zeros_lora, w_iclr_up[0], w_gate_up[0]], axis=0)
    x2 = _mixer(x1, n_batch, vecs, w_in[0].astype(BF), w_out_a[0], lora, w_out_b[0], w_o[0])

    out = _ffn(x2, row(ffn2_norm[0]), ffn2_w_gate[0], ffn2_w_up[0], ffn2_w_down[0],
               final_gain=row(final_norm))
    return out.reshape(n_batch, seq, d)
```

```python
import functools

import jax
import jax.numpy as jnp
from jax import lax
from jax.experimental import pallas as pl
from jax.experimental.pallas import tpu as pltpu

F32 = jnp.float32
BF = jnp.bfloat16

D_MODEL = 1024
D_CONV = 512
D_RWKV = 512
HEAD = 64
D_FF = 2816
COLS_A = 3 * D_CONV
COLS_B = 3 * D_RWKV + 64 + 64 + 128
OFF_B = COLS_A
OFF_GA = COLS_A + COLS_B
OFF_GB = OFF_GA + D_MODEL
RMS_EPS = 1e-6
GN_EPS = 64e-5
(VEC_MU, VEC_GAIN, VEC_W0, VEC_A0, VEC_KK, VEC_KA, VEC_RK, VEC_LNW, VEC_LNB, VEC_CONV) = range(10)
VEC_ROWS = 16
LORA_ROWS = 128
DECAY_SCALE = 0.6065306597126334

CHUNK = 64
PAIR = 2 * HEAD
N_PAIR = D_RWKV // PAIR
HALO = 8
TM_FFN = 512
TF_FFN = 256
TM_MIX = 512
CHUNK_GROUP = 4
GATE_COLS = 256
VMEM_LIMIT = 56 * 1024 * 1024


def _dot(a, b):
    return jnp.dot(a.astype(BF), b.astype(BF), preferred_element_type=F32)


def _dot_nt(a, b):
    return lax.dot_general(a.astype(BF), b.astype(BF), (((1,), (1,)), ((), ())),
                           preferred_element_type=F32)


def _dot_f32(a, b):
    return jnp.dot(a, b, preferred_element_type=F32)


def _split2(x):
    hi = x.astype(BF)
    lo = (x - hi.astype(F32)).astype(BF)
    return hi, lo


def _dot3(a, b):
    ah, al = _split2(a)
    bh, bl = _split2(b)
    return (jnp.dot(jnp.concatenate([ah, al], axis=1), jnp.concatenate([bh, bh], axis=0),
                    preferred_element_type=F32)
            + jnp.dot(ah, bl, preferred_element_type=F32))


def _rms_norm(x, gain):
    return x * lax.rsqrt(jnp.mean(x * x, axis=-1, keepdims=True) + RMS_EPS) * gain


def _ffn_body(*refs, final_norm):
    if final_norm:
        x_ref, gain_ref, wg_ref, wu_ref, wd_ref, fn_ref, o_ref, act_ref = refs
    else:
        x_ref, gain_ref, wg_ref, wu_ref, wd_ref, o_ref, act_ref = refs
    x = x_ref[...]
    h = _rms_norm(x, gain_ref[...])
    for c in range(D_FF // TF_FFN):
        sl = slice(c * TF_FFN, (c + 1) * TF_FFN)
        g = _dot_f32(h, wg_ref[:, sl])
        u = _dot_f32(h, wu_ref[:, sl])
        act_ref[:, sl] = g * jax.nn.sigmoid(g) * u
    y = x + 0.5 * _dot_f32(act_ref[...], wd_ref[...])
    if final_norm:
        y = _rms_norm(y, fn_ref[...])
    o_ref[...] = y


def _const_spec(shape):
    return pl.BlockSpec(shape, lambda *_: (0,) * len(shape), pipeline_mode=pl.Buffered(1))


def _ffn(x, gain, wg, wu, wd, final_gain=None):
    m = x.shape[0]
    final_norm = final_gain is not None
    in_specs = [
        pl.BlockSpec((TM_FFN, D_MODEL), lambda i: (i, 0)),
        _const_spec((1, D_MODEL)),
        _const_spec((D_MODEL, D_FF)),
        _const_spec((D_MODEL, D_FF)),
        _const_spec((D_FF, D_MODEL)),
    ]
    args = [x, gain, wg, wu, wd]
    if final_norm:
        in_specs.append(_const_spec((1, D_MODEL)))
        args.append(final_gain)
    return pl.pallas_call(
        functools.partial(_ffn_body, final_norm=final_norm),
        grid=(m // TM_FFN,),
        in_specs=in_specs,
        out_specs=pl.BlockSpec((TM_FFN, D_MODEL), lambda i: (i, 0)),
        out_shape=jax.ShapeDtypeStruct((m, D_MODEL), F32),
        scratch_shapes=[pltpu.VMEM((TM_FFN, D_FF), F32)],
        compiler_params=pltpu.CompilerParams(
            dimension_semantics=("arbitrary",), vmem_limit_bytes=VMEM_LIMIT),
        name="ffn_final" if final_norm else "ffn",
    )(*args)


def _pair_index():
    row = lax.broadcasted_iota(jnp.int32, (PAIR, PAIR), 0)
    lane = lax.broadcasted_iota(jnp.int32, (PAIR, PAIR), 1)
    return row, lane


def _interleave(*gens):
    live = list(gens)
    while live:
        for g in list(live):
            try:
                next(g)
            except StopIteration:
                live.remove(g)


def _tri_inverse_stages(l_bds, out):
    row, lane = _pair_index()
    zero = jnp.zeros((PAIR, PAIR), F32)
    eye = jnp.where(row == lane, 1.0, 0.0).astype(F32)

    def off(m):
        return ((row // (2 * m)) == (lane // (2 * m))) & ((row % (2 * m)) >= m) & ((lane % (2 * m)) < m)

    ts = [eye + jnp.where(off(1), l, zero) for l in l_bds]
    m = 2
    while m < CHUNK:
        mask = off(m)
        tl = [_dot(t, jnp.where(mask, l, zero)) for t, l in zip(ts, l_bds)]
        yield
        ts = [t + _dot(x, t) for t, x in zip(ts, tl)]
        yield
        m *= 2
    out.extend(ts)


def _wkv_tables_stages(chunks, tril, out):
    c = CHUNK
    head0 = lax.broadcasted_iota(jnp.int32, (c, PAIR), 1) < HEAD
    row, lane = _pair_index()
    t_idx = row % c
    s_idx = lane % c
    blockdiag = (row // c) == (lane // c)
    zero = jnp.zeros((c, PAIR), F32)
    zero2 = jnp.zeros((PAIR, PAIR), F32)

    e_tots, lhs0, lhs1, rhs0, rhs1, at_bd, rt_bd, bkh, v_swap = [], [], [], [], [], [], [], [], []
    for r, k, v, a, b, lw in chunks:
        h1 = lw.astype(BF)
        r1 = lw - h1.astype(F32)
        h2 = r1.astype(BF)
        h3 = (r1 - h2.astype(F32)).astype(BF)
        cs = jnp.dot(tril, jnp.concatenate([h1, h2, h3], axis=0), preferred_element_type=F32)
        tot = cs[c - 1:c, :]
        e_tots.append(jnp.exp(tot))
        at = a * jnp.exp(cs - lw)
        rt = r * jnp.exp(cs)
        e_n = jnp.exp(-cs)
        bt = b * e_n
        kt = k * e_n
        bh = bt * e_tots[-1]
        kh = kt * e_tots[-1]
        for j in range(N_PAIR):
            sl = slice(j * PAIR, (j + 1) * PAIR)
            ar0 = jnp.concatenate([jnp.where(head0, at[:, sl], zero), jnp.where(head0, rt[:, sl], zero)], axis=0)
            ar1 = jnp.concatenate([jnp.where(head0, zero, at[:, sl]), jnp.where(head0, zero, rt[:, sl])], axis=0)
            lhs0.append(ar0)
            lhs1.append(ar1)
            rhs0.append(jnp.concatenate([bt[:, sl], kt[:, sl]], axis=0))
            rhs1.append(jnp.concatenate([kt[:, sl], bt[:, sl]], axis=0))
            at_bd.append(jnp.concatenate([ar0[:c], ar1[:c]], axis=0))
            rt_bd.append(jnp.concatenate([ar0[c:], ar1[c:]], axis=0))
            bkh.append(jnp.concatenate([bh[:, sl], kh[:, sl]], axis=0))
            v_swap.append(jnp.concatenate([jnp.where(head0, zero, v[:, sl]),
                                           jnp.where(head0, v[:, sl], zero)], axis=0))
    yield
    g0 = [_dot_nt(x, y) for x, y in zip(lhs0, rhs0)]
    g1 = [_dot_nt(x, y) for x, y in zip(lhs1, rhs1)]
    yield
    aa = [jnp.where(s_idx < t_idx, jnp.concatenate([x[:c], y[:c]], axis=0), zero2) for x, y in zip(g0, g1)]
    arr = [jnp.where(s_idx <= t_idx, jnp.concatenate([x[c:], y[c:]], axis=0), zero2) for x, y in zip(g0, g1)]
    l_bd = [jnp.where(blockdiag, x, zero2) for x in aa]
    ak_ad = [jnp.where(blockdiag, zero2, x) for x in aa]
    akv = [_dot(x, y) for x, y in zip(ak_ad, v_swap)]
    yield
    t_inv = []
    yield from _tri_inverse_stages(l_bd, t_inv)
    t_at_akv = [_dot(t, jnp.concatenate([x, w], axis=1)) for t, x, w in zip(t_inv, at_bd, akv)]
    yield
    tar = [jnp.concatenate([z[:, :PAIR], y], axis=0) for z, y in zip(t_at_akv, rt_bd)]
    u_v = [z[:, PAIR:] for z in t_at_akv]
    for i in range(len(chunks)):
        sl = slice(i * N_PAIR, (i + 1) * N_PAIR)
        out.append(((tar[sl], u_v[sl], arr[sl], bkh[sl], v_swap[sl]), e_tots[i]))


def _wkv_apply_stages(tables, v, e_tot, states, out):
    c = CHUNK
    tar, u_v, arr, bkh, v_swap = tables
    head0 = lax.broadcasted_iota(jnp.int32, (c, PAIR), 1) < HEAD
    row, lane = _pair_index()
    blockdiag = (row // c) == (lane // c)
    zero2 = jnp.zeros((PAIR, PAIR), F32)
    xs = [_dot_nt(x, s) for x, s in zip(tar, states)]
    yield
    u_bd = [z[:PAIR] + w for z, w in zip(xs, u_v)]
    rs_bd = [z[PAIR:] for z in xs]
    y_bd = [_dot(m, u + w) + z for m, u, w, z in zip(arr, u_bd, v_swap, rs_bd)]
    ys = [jnp.where(head0, z[:c], z[c:]) for z in y_bd]
    new_states = []
    for j in range(N_PAIR):
        sl = slice(j * PAIR, (j + 1) * PAIR)
        uv = jnp.concatenate([u_bd[j][:c] + u_bd[j][c:], v[:, sl]], axis=0)
        upd = _dot(uv.T, bkh[j])
        new_states.append(jnp.where(blockdiag, states[j] * e_tot[:, sl] + upd, zero2))
    yield
    out.append((jnp.concatenate(ys, axis=1), new_states))


def _mixer_body(x_ref, vec_ref, win_ref, wouta_ref, lora_ref, woutb_ref, wo_ref,
                o_ref,
                cu_buf, pb_buf, r_s, k_s, v_s, a_s, b_s, lw_s, g_s, bon_s, y_s, m_s, st_s):
    tm = TM_MIX

    @pl.when(pl.program_id(1) == 0)
    def _():
        cu_buf[0:HALO, :] = jnp.zeros((HALO, D_CONV), F32)
        pb_buf[0:HALO, :] = jnp.zeros((HALO, COLS_B), F32)
        st_s[...] = jnp.zeros_like(st_s)

    def vec(row, n, rows=1):
        return vec_ref[row:row + rows, 0:n]

    x = x_ref[...]
    h = _rms_norm(x, vec(VEC_GAIN, D_MODEL)).astype(BF)

    def head_sum(z):
        n = z.shape[0]
        first = lax.broadcasted_iota(jnp.int32, (n, PAIR), 1) < HEAD
        outs = []
        for j in range(N_PAIR):
            t = z[:, j * PAIR:(j + 1) * PAIR]
            s0 = jnp.sum(jnp.where(first, t, 0.0), axis=-1, keepdims=True)
            s1 = jnp.sum(jnp.where(first, 0.0, t), axis=-1, keepdims=True)
            outs.append(jnp.where(first, s0, s1))
        return jnp.concatenate(outs, axis=1)

    def proj(lo, hi):
        return jnp.dot(h, win_ref[:, lo:hi], preferred_element_type=F32)

    pa_parts = []

    def branch_a_proj_stages():
        for p in range(COLS_A // D_CONV):
            pa_parts.append(proj(p * D_CONV, (p + 1) * D_CONV))
            yield

    def branch_b_prep_stages():
        pb = proj(OFF_B, OFF_GA)
        yield
        pb_buf[HALO:HALO + tm, :] = pb
        prev = pb_buf[pl.ds(HALO - 1, tm), :]
        pb_buf[0:HALO, :] = pb_buf[tm:tm + HALO, :]
        pbm = pb + (prev - pb) * vec(VEC_MU, COLS_B)
        r = pbm[:, 0:D_RWKV]
        k = pbm[:, D_RWKV:2 * D_RWKV]
        v = pbm[:, 2 * D_RWKV:3 * D_RWKV]
        xwa = pbm[:, 3 * D_RWKV:3 * D_RWKV + 128]
        xg = pbm[:, 3 * D_RWKV + 128:COLS_B]
        z = vec(VEC_W0, D_RWKV) + _dot3(jnp.tanh(xwa), lora_ref[0:LORA_ROWS, :])
        lw_s[...] = -DECAY_SCALE * jax.nn.sigmoid(z)
        iclr = jax.nn.sigmoid(vec(VEC_A0, D_RWKV) + _dot3(xwa, lora_ref[LORA_ROWS:2 * LORA_ROWS, :]))
        g_s[...] = _dot_f32(jax.nn.sigmoid(xg), lora_ref[2 * LORA_ROWS:3 * LORA_ROWS, :])
        yield
        kk = k * vec(VEC_KK, D_RWKV)
        kk = kk * lax.rsqrt(jnp.maximum(head_sum(kk * kk), 1e-24))
        k2 = k * (1.0 + (iclr - 1.0) * vec(VEC_KA, D_RWKV))
        r_s[...] = r
        k_s[...] = k2
        v_s[...] = v
        a_s[...] = -kk
        b_s[...] = kk * iclr
        yield
        bon_s[...] = head_sum(r * k2 * vec(VEC_RK, D_RWKV)) * v
        yield

    _interleave(branch_b_prep_stages(), branch_a_proj_stages())

    cu = pa_parts[1] * pa_parts[2]
    cu_buf[HALO:HALO + tm, :] = cu
    cw = vec(VEC_CONV, D_CONV, rows=3)
    conv = (cw[2:3] * cu + cw[1:2] * cu_buf[pl.ds(HALO - 1, tm), :]
            + cw[0:1] * cu_buf[pl.ds(HALO - 2, tm), :])
    cu_buf[0:HALO, :] = cu_buf[tm:tm + HALO, :]
    gated = pa_parts[0] * conv

    def branch_a_out_stages():
        for q in range(D_MODEL // GATE_COLS):
            cols = slice(q * GATE_COLS, (q + 1) * GATE_COLS)
            ya = _dot_f32(gated, wouta_ref[:, cols])
            ga = proj(OFF_GA + q * GATE_COLS, OFF_GA + (q + 1) * GATE_COLS)
            m_s[:, cols] = jax.nn.sigmoid(ga) * ya
            yield

    group_rows = CHUNK_GROUP * CHUNK
    n_group = tm // group_rows
    sgb_parts = [[] for _ in range(n_group)]

    def gate_b_stages(gi):
        rows = slice(gi * group_rows, (gi + 1) * group_rows)
        for q in range(D_MODEL // GATE_COLS):
            lo = OFF_GB + q * GATE_COLS
            gb = jnp.dot(h[rows], win_ref[:, lo:lo + GATE_COLS], preferred_element_type=F32)
            sgb_parts[gi].append(jax.nn.sigmoid(gb))
            yield

    def post_stages(gi):
        rows = slice(gi * group_rows, (gi + 1) * group_rows)
        y = y_s[rows, :]
        mean = head_sum(y) * (1.0 / HEAD)
        yield
        d = y - mean
        var = head_sum(d * d) * (1.0 / HEAD)
        yield
        yn = (d * lax.rsqrt(var + GN_EPS) * vec(VEC_LNW, D_RWKV) + vec(VEC_LNB, D_RWKV)
              + bon_s[rows, :])
        yb = _dot_f32(yn * g_s[rows, :], woutb_ref[...])
        yield
        merged = m_s[rows, :] + jnp.concatenate(sgb_parts[gi], axis=1) * yb
        o_ref[rows, :] = x[rows] + _dot_f32(merged, wo_ref[...])
        yield

    tril = jnp.where(lax.broadcasted_iota(jnp.int32, (CHUNK, 3 * CHUNK), 1) % CHUNK
                     <= lax.broadcasted_iota(jnp.int32, (CHUNK, 3 * CHUNK), 0), 1.0, 0.0).astype(BF)

    def chunk_rows(ci):
        return slice(ci * CHUNK, (ci + 1) * CHUNK)

    group_tables = [[] for _ in range(n_group)]
    states = [[st_s[j] for j in range(N_PAIR)]]

    def tables_stages(gi):
        chunks = []
        for ci in range(gi * CHUNK_GROUP, (gi + 1) * CHUNK_GROUP):
            rows = chunk_rows(ci)
            chunks.append((r_s[rows, :], k_s[rows, :], v_s[rows, :], a_s[rows, :], b_s[rows, :],
                           lw_s[rows, :]))
        yield from _wkv_tables_stages(chunks, tril, group_tables[gi])

    def apply_stages(gi):
        for i in range(CHUNK_GROUP):
            rows = chunk_rows(gi * CHUNK_GROUP + i)
            tables, e_tot = group_tables[gi][i]
            res = []
            yield from _wkv_apply_stages(tables, v_s[rows, :], e_tot, states[0], res)
            y_s[rows, :] = res[0][0]
            states[0] = res[0][1]

    _interleave(tables_stages(0), branch_a_out_stages())
    for gi in range(n_group):
        stages = [apply_stages(gi), gate_b_stages(gi)]
        if gi + 1 < n_group:
            stages.insert(0, tables_stages(gi + 1))
        if gi > 0:
            stages.append(post_stages(gi - 1))
        _interleave(*stages)
    for j in range(N_PAIR):
        st_s[j] = states[0][j]
    _interleave(post_stages(n_group - 1))


def _mixer(x, n_batch, vecs, win, wouta, lora, woutb, wo):
    m = x.shape[0]
    tiles = m // n_batch // TM_MIX
    consts = [vecs, win, wouta, lora, woutb, wo]
    tok_spec = pl.BlockSpec((TM_MIX, D_MODEL), lambda bi, ti: (bi * tiles + ti, 0))
    vec = lambda n: pltpu.VMEM((TM_MIX, n), F32)
    return pl.pallas_call(
        _mixer_body,
        grid=(n_batch, tiles),
        in_specs=[tok_spec] + [_const_spec(c.shape) for c in consts],
        out_specs=tok_spec,
        out_shape=jax.ShapeDtypeStruct((m, D_MODEL), F32),
        scratch_shapes=[
            pltpu.VMEM((TM_MIX + HALO, D_CONV), F32),
            pltpu.VMEM((TM_MIX + HALO, COLS_B), F32),
            vec(D_RWKV), vec(D_RWKV), vec(D_RWKV), vec(D_RWKV), vec(D_RWKV), vec(D_RWKV),
            vec(D_RWKV), vec(D_RWKV), vec(D_RWKV),
            vec(D_MODEL),
            pltpu.VMEM((N_PAIR, PAIR, PAIR), F32),
        ],
        compiler_params=pltpu.CompilerParams(
            dimension_semantics=("arbitrary", "arbitrary"), vmem_limit_bytes=VMEM_LIMIT),
        name="mixer",
    )(x, *consts)


def kernel(x, ffn1_norm, ffn1_w_gate, ffn1_w_up, ffn1_w_down, mix_norm, w_in, conv_w, w_out_a, mu_b, w0, w_decay_up, a0, w_iclr_up, w_gate_up, k_k, k_a, r_k, ln_x_w, ln_x_b, w_out_b, w_o, ffn2_norm, ffn2_w_gate, ffn2_w_up, ffn2_w_down, final_norm):
    n_batch, seq, d = x.shape
    assert d == D_MODEL and seq % TM_MIX == 0 and (n_batch * seq) % TM_FFN == 0
    assert ffn1_norm.shape[0] == 1, "single layer"
    row = lambda t: t.reshape(1, -1).astype(F32)
    xf = x.reshape(n_batch * seq, d)

    x1 = _ffn(xf, row(ffn1_norm[0]), ffn1_w_gate[0], ffn1_w_up[0], ffn1_w_down[0])

    rows = [mu_b[0], mix_norm[0], w0[0], a0[0], k_k[0], k_a[0], r_k[0], ln_x_w[0], ln_x_b[0],
            conv_w[0][0], conv_w[0][1], conv_w[0][2]]
    pieces = []
    for t in rows:
        t = t.reshape(-1).astype(F32)
        pieces += [t, jnp.zeros((COLS_B - t.size,), F32)]
    pieces.append(jnp.zeros(((VEC_ROWS - len(rows)) * COLS_B,), F32))
    vecs = jnp.concatenate(pieces).reshape(VEC_ROWS, COLS_B)
    zeros_lora = jnp.zeros((64, D_RWKV), F32)
    lora = jnp.concatenate([w_decay_up[0], zeros_lora, zeros_lora, w_iclr_up[0], w_gate_up[0]], axis=0)
    x2 = _mixer(x1, n_batch, vecs, w_in[0].astype(BF), w_out_a[0], lora, w_out_b[0], w_o[0])

    out = _ffn(x2, row(ffn2_norm[0]), ffn2_w_gate[0], ffn2_w_up[0], ffn2_w_down[0],
               final_gain=row(final_norm))
    return out.reshape(n_batch, seq, d)
```

```python
import functools

import jax
import jax.numpy as jnp
from jax import lax
from jax.experimental import pallas as pl
from jax.experimental.pallas import tpu as pltpu

F32 = jnp.float32
BF = jnp.bfloat16

D_MODEL = 1024
D_CONV = 512
D_RWKV = 512
HEAD = 64
D_FF = 2816
COLS_A = 3 * D_CONV
COLS_B = 3 * D_RWKV + 64 + 64 + 128
OFF_B = COLS_A
OFF_GA = COLS_A + COLS_B
OFF_GB = OFF_GA + D_MODEL
RMS_EPS = 1e-6
GN_EPS = 64e-5
(VEC_MU, VEC_GAIN, VEC_W0, VEC_A0, VEC_KK, VEC_KA, VEC_RK, VEC_LNW, VEC_LNB, VEC_CONV) = range(10)
VEC_ROWS = 16
LORA_ROWS = 128
DECAY_SCALE = 0.6065306597126334

CHUNK = 64
PAIR = 2 * HEAD
N_PAIR = D_RWKV // PAIR
HALO = 8
TM_FFN = 512
TF_FFN = 256
TM_MIX = 512
CHUNK_GROUP = 4
GATE_COLS = 256
VMEM_LIMIT = 56 * 1024 * 1024


def _dot(a, b):
    return jnp.dot(a.astype(BF), b.astype(BF), preferred_element_type=F32)


def _dot_nt(a, b):
    return lax.dot_general(a.astype(BF), b.astype(BF), (((1,), (1,)), ((), ())),
                           preferred_element_type=F32)


def _dot_f32(a, b):
    return jnp.dot(a, b, preferred_element_type=F32)


def _split2(x):
    hi = x.astype(BF)
    lo = (x - hi.astype(F32)).astype(BF)
    return hi, lo


def _dot3(a, b):
    ah, al = _split2(a)
    bh, bl = _split2(b)
    return (jnp.dot(jnp.concatenate([ah, al], axis=1), jnp.concatenate([bh, bh], axis=0),
                    preferred_element_type=F32)
            + jnp.dot(ah, bl, preferred_element_type=F32))


def _rms_norm(x, gain):
    return x * lax.rsqrt(jnp.mean(x * x, axis=-1, keepdims=True) + RMS_EPS) * gain


def _ffn_body(*refs, final_norm, with_cast):
    refs = list(refs)
    x_ref, gain_ref, wg_ref, wu_ref, wd_ref = refs[:5]
    rest = refs[5:]
    fn_ref = rest.pop(0) if final_norm else None
    cast_in_ref = rest.pop(0) if with_cast else None
    o_ref = rest.pop(0)
    cast_out_ref = rest.pop(0) if with_cast else None
    act_ref, = rest
    if with_cast:
        cast_out_ref[...] = cast_in_ref[...].astype(BF)
    x = x_ref[...]
    h = _rms_norm(x, gain_ref[...])
    for c in range(D_FF // TF_FFN):
        sl = slice(c * TF_FFN, (c + 1) * TF_FFN)
        g = _dot_f32(h, wg_ref[:, sl])
        u = _dot_f32(h, wu_ref[:, sl])
        act_ref[:, sl] = g * jax.nn.sigmoid(g) * u
    y = x + 0.5 * _dot_f32(act_ref[...], wd_ref[...])
    if final_norm:
        y = _rms_norm(y, fn_ref[...])
    o_ref[...] = y


def _const_spec(shape):
    return pl.BlockSpec(shape, lambda *_: (0,) * len(shape), pipeline_mode=pl.Buffered(1))


def _ffn(x, gain, wg, wu, wd, final_gain=None, cast_to_bf16=None):
    m = x.shape[0]
    steps = m // TM_FFN
    final_norm = final_gain is not None
    with_cast = cast_to_bf16 is not None
    tok_spec = pl.BlockSpec((TM_FFN, D_MODEL), lambda i: (i, 0))
    in_specs = [tok_spec, _const_spec((1, D_MODEL)), _const_spec((D_MODEL, D_FF)),
                _const_spec((D_MODEL, D_FF)), _const_spec((D_FF, D_MODEL))]
    args = [x, gain, wg, wu, wd]
    out_specs = tok_spec
    out_shape = jax.ShapeDtypeStruct((m, D_MODEL), F32)
    if final_norm:
        in_specs.append(_const_spec((1, D_MODEL)))
        args.append(final_gain)
    if with_cast:
        rows, cols = cast_to_bf16.shape
        assert rows % steps == 0
        cast_spec = pl.BlockSpec((rows // steps, cols), lambda i: (i, 0))
        in_specs.append(cast_spec)
        args.append(cast_to_bf16)
        out_specs = (tok_spec, cast_spec)
        out_shape = (out_shape, jax.ShapeDtypeStruct((rows, cols), BF))
    return pl.pallas_call(
        functools.partial(_ffn_body, final_norm=final_norm, with_cast=with_cast),
        grid=(steps,),
        in_specs=in_specs,
        out_specs=out_specs,
        out_shape=out_shape,
        scratch_shapes=[pltpu.VMEM((TM_FFN, D_FF), F32)],
        compiler_params=pltpu.CompilerParams(
            dimension_semantics=("arbitrary",), vmem_limit_bytes=VMEM_LIMIT),
        name="ffn_final" if final_norm else "ffn",
    )(*args)


def _pair_index():
    row = lax.broadcasted_iota(jnp.int32, (PAIR, PAIR), 0)
    lane = lax.broadcasted_iota(jnp.int32, (PAIR, PAIR), 1)
    return row, lane


def _interleave(*gens):
    live = list(gens)
    while live:
        for g in list(live):
            try:
                next(g)
            except StopIteration:
                live.remove(g)


def _tri_inverse_stages(l_bds, out):
    row, lane = _pair_index()
    zero = jnp.zeros((PAIR, PAIR), F32)
    eye = jnp.where(row == lane, 1.0, 0.0).astype(F32)

    def off(m):
        return ((row // (2 * m)) == (lane // (2 * m))) & ((row % (2 * m)) >= m) & ((lane % (2 * m)) < m)

    ts = [eye + jnp.where(off(1), l, zero) for l in l_bds]
    m = 2
    while m < CHUNK:
        mask = off(m)
        tl = [_dot(t, jnp.where(mask, l, zero)) for t, l in zip(ts, l_bds)]
        yield
        ts = [t + _dot(x, t) for t, x in zip(ts, tl)]
        yield
        m *= 2
    out.extend(ts)


def _wkv_tables_stages(chunks, tril, out):
    c = CHUNK
    head0 = lax.broadcasted_iota(jnp.int32, (c, PAIR), 1) < HEAD
    row, lane = _pair_index()
    t_idx = row % c
    s_idx = lane % c
    blockdiag = (row // c) == (lane // c)
    zero = jnp.zeros((c, PAIR), F32)
    zero2 = jnp.zeros((PAIR, PAIR), F32)

    e_tots, lhs0, lhs1, rhs0, rhs1, at_bd, rt_bd, bkh, v_swap = [], [], [], [], [], [], [], [], []
    for r, k, v, a, b, lw in chunks:
        h1 = lw.astype(BF)
        r1 = lw - h1.astype(F32)
        h2 = r1.astype(BF)
        h3 = (r1 - h2.astype(F32)).astype(BF)
        cs = jnp.dot(tril, jnp.concatenate([h1, h2, h3], axis=0), preferred_element_type=F32)
        tot = cs[c - 1:c, :]
        e_tots.append(jnp.exp(tot))
        at = a * jnp.exp(cs - lw)
        rt = r * jnp.exp(cs)
        e_n = jnp.exp(-cs)
        bt = b * e_n
        kt = k * e_n
        bh = bt * e_tots[-1]
        kh = kt * e_tots[-1]
        for j in range(N_PAIR):
            sl = slice(j * PAIR, (j + 1) * PAIR)
            ar0 = jnp.concatenate([jnp.where(head0, at[:, sl], zero), jnp.where(head0, rt[:, sl], zero)], axis=0)
            ar1 = jnp.concatenate([jnp.where(head0, zero, at[:, sl]), jnp.where(head0, zero, rt[:, sl])], axis=0)
            lhs0.append(ar0)
            lhs1.append(ar1)
            rhs0.append(jnp.concatenate([bt[:, sl], kt[:, sl]], axis=0))
            rhs1.append(jnp.concatenate([kt[:, sl], bt[:, sl]], axis=0))
            at_bd.append(jnp.concatenate([ar0[:c], ar1[:c]], axis=0))
            rt_bd.append(jnp.concatenate([ar0[c:], ar1[c:]], axis=0))
            bkh.append(jnp.concatenate([bh[:, sl], kh[:, sl]], axis=0))
            v_swap.append(jnp.concatenate([jnp.where(head0, zero, v[:, sl]),
                                           jnp.where(head0, v[:, sl], zero)], axis=0))
    yield
    g0 = [_dot_nt(x, y) for x, y in zip(lhs0, rhs0)]
    g1 = [_dot_nt(x, y) for x, y in zip(lhs1, rhs1)]
    yield
    aa = [jnp.where(s_idx < t_idx, jnp.concatenate([x[:c], y[:c]], axis=0), zero2) for x, y in zip(g0, g1)]
    arr = [jnp.where(s_idx <= t_idx, jnp.concatenate([x[c:], y[c:]], axis=0), zero2) for x, y in zip(g0, g1)]
    l_bd = [jnp.where(blockdiag, x, zero2) for x in aa]
    ak_ad = [jnp.where(blockdiag, zero2, x) for x in aa]
    akv = [_dot(x, y) for x, y in zip(ak_ad, v_swap)]
    yield
    t_inv = []
    yield from _tri_inverse_stages(l_bd, t_inv)
    t_at_akv = [_dot(t, jnp.concatenate([x, w], axis=1)) for t, x, w in zip(t_inv, at_bd, akv)]
    yield
    tar = [jnp.concatenate([z[:, :PAIR], y], axis=0) for z, y in zip(t_at_akv, rt_bd)]
    u_v = [z[:, PAIR:] for z in t_at_akv]
    for i in range(len(chunks)):
        sl = slice(i * N_PAIR, (i + 1) * N_PAIR)
        out.append(((tar[sl], u_v[sl], arr[sl], bkh[sl], v_swap[sl]), e_tots[i]))


def _wkv_apply_stages(tables, v, e_tot, states, out):
    c = CHUNK
    tar, u_v, arr, bkh, v_swap = tables
    head0 = lax.broadcasted_iota(jnp.int32, (c, PAIR), 1) < HEAD
    row, lane = _pair_index()
    blockdiag = (row // c) == (lane // c)
    zero2 = jnp.zeros((PAIR, PAIR), F32)
    xs = [_dot_nt(x, s) for x, s in zip(tar, states)]
    yield
    u_bd = [z[:PAIR] + w for z, w in zip(xs, u_v)]
    rs_bd = [z[PAIR:] for z in xs]
    y_bd = [_dot(m, u + w) + z for m, u, w, z in zip(arr, u_bd, v_swap, rs_bd)]
    ys = [jnp.where(head0, z[:c], z[c:]) for z in y_bd]
    new_states = []
    for j in range(N_PAIR):
        sl = slice(j * PAIR, (j + 1) * PAIR)
        uv = jnp.concatenate([u_bd[j][:c] + u_bd[j][c:], v[:, sl]], axis=0)
        upd = _dot(uv.T, bkh[j])
        new_states.append(jnp.where(blockdiag, states[j] * e_tot[:, sl] + upd, zero2))
    yield
    out.append((jnp.concatenate(ys, axis=1), new_states))


def _mixer_body(x_ref, vec_ref, win_ref, wouta_ref, lora_ref, woutb_ref, wo_ref,
                o_ref,
                cu_buf, pb_buf, r_s, k_s, v_s, a_s, b_s, lw_s, g_s, bon_s, y_s, m_s, st_s):
    tm = TM_MIX

    @pl.when(pl.program_id(1) == 0)
    def _():
        cu_buf[0:HALO, :] = jnp.zeros((HALO, D_CONV), F32)
        pb_buf[0:HALO, :] = jnp.zeros((HALO, COLS_B), F32)
        st_s[...] = jnp.zeros_like(st_s)

    def vec(row, n, rows=1):
        return vec_ref[row:row + rows, 0:n]

    x = x_ref[...]
    h = _rms_norm(x, vec(VEC_GAIN, D_MODEL)).astype(BF)

    def head_sum(z):
        n = z.shape[0]
        first = lax.broadcasted_iota(jnp.int32, (n, PAIR), 1) < HEAD
        outs = []
        for j in range(N_PAIR):
            t = z[:, j * PAIR:(j + 1) * PAIR]
            s0 = jnp.sum(jnp.where(first, t, 0.0), axis=-1, keepdims=True)
            s1 = jnp.sum(jnp.where(first, 0.0, t), axis=-1, keepdims=True)
            outs.append(jnp.where(first, s0, s1))
        return jnp.concatenate(outs, axis=1)

    def proj(lo, hi):
        return jnp.dot(h, win_ref[:, lo:hi], preferred_element_type=F32)

    pa_parts = []

    def branch_a_proj_stages():
        for p in range(COLS_A // D_CONV):
            pa_parts.append(proj(p * D_CONV, (p + 1) * D_CONV))
            yield

    def branch_b_prep_stages():
        pb = proj(OFF_B, OFF_GA)
        yield
        pb_buf[HALO:HALO + tm, :] = pb
        prev = pb_buf[pl.ds(HALO - 1, tm), :]
        pb_buf[0:HALO, :] = pb_buf[tm:tm + HALO, :]
        pbm = pb + (prev - pb) * vec(VEC_MU, COLS_B)
        r = pbm[:, 0:D_RWKV]
        k = pbm[:, D_RWKV:2 * D_RWKV]
        v = pbm[:, 2 * D_RWKV:3 * D_RWKV]
        xwa = pbm[:, 3 * D_RWKV:3 * D_RWKV + 128]
        xg = pbm[:, 3 * D_RWKV + 128:COLS_B]
        z = vec(VEC_W0, D_RWKV) + _dot3(jnp.tanh(xwa), lora_ref[0:LORA_ROWS, :])
        lw_s[...] = -DECAY_SCALE * jax.nn.sigmoid(z)
        iclr = jax.nn.sigmoid(vec(VEC_A0, D_RWKV) + _dot3(xwa, lora_ref[LORA_ROWS:2 * LORA_ROWS, :]))
        g_s[...] = _dot_f32(jax.nn.sigmoid(xg), lora_ref[2 * LORA_ROWS:3 * LORA_ROWS, :])
        yield
        kk = k * vec(VEC_KK, D_RWKV)
        kk = kk * lax.rsqrt(jnp.maximum(head_sum(kk * kk), 1e-24))
        k2 = k * (1.0 + (iclr - 1.0) * vec(VEC_KA, D_RWKV))
        r_s[...] = r
        k_s[...] = k2
        v_s[...] = v
        a_s[...] = -kk
        b_s[...] = kk * iclr
        yield
        bon_s[...] = head_sum(r * k2 * vec(VEC_RK, D_RWKV)) * v
        yield

    _interleave(branch_b_prep_stages(), branch_a_proj_stages())

    cu = pa_parts[1] * pa_parts[2]
    cu_buf[HALO:HALO + tm, :] = cu
    cw = vec(VEC_CONV, D_CONV, rows=3)
    conv = (cw[2:3] * cu + cw[1:2] * cu_buf[pl.ds(HALO - 1, tm), :]
            + cw[0:1] * cu_buf[pl.ds(HALO - 2, tm), :])
    cu_buf[0:HALO, :] = cu_buf[tm:tm + HALO, :]
    gated = pa_parts[0] * conv

    def branch_a_out_stages():
        for q in range(D_MODEL // GATE_COLS):
            cols = slice(q * GATE_COLS, (q + 1) * GATE_COLS)
            ya = _dot_f32(gated, wouta_ref[:, cols])
            ga = proj(OFF_GA + q * GATE_COLS, OFF_GA + (q + 1) * GATE_COLS)
            m_s[:, cols] = jax.nn.sigmoid(ga) * ya
            yield

    group_rows = CHUNK_GROUP * CHUNK
    n_group = tm // group_rows
    sgb_parts = [[] for _ in range(n_group)]

    def gate_b_stages(gi):
        rows = slice(gi * group_rows, (gi + 1) * group_rows)
        for q in range(D_MODEL // GATE_COLS):
            lo = OFF_GB + q * GATE_COLS
            gb = jnp.dot(h[rows], win_ref[:, lo:lo + GATE_COLS], preferred_element_type=F32)
            sgb_parts[gi].append(jax.nn.sigmoid(gb))
            yield

    def post_stages(gi):
        rows = slice(gi * group_rows, (gi + 1) * group_rows)
        y = y_s[rows, :]
        mean = head_sum(y) * (1.0 / HEAD)
        yield
        d = y - mean
        var = head_sum(d * d) * (1.0 / HEAD)
        yield
        yn = (d * lax.rsqrt(var + GN_EPS) * vec(VEC_LNW, D_RWKV) + vec(VEC_LNB, D_RWKV)
              + bon_s[rows, :])
        yb = _dot_f32(yn * g_s[rows, :], woutb_ref[...])
        yield
        merged = m_s[rows, :] + jnp.concatenate(sgb_parts[gi], axis=1) * yb
        o_ref[rows, :] = x[rows] + _dot_f32(merged, wo_ref[...])
        yield

    tril = jnp.where(lax.broadcasted_iota(jnp.int32, (CHUNK, 3 * CHUNK), 1) % CHUNK
                     <= lax.broadcasted_iota(jnp.int32, (CHUNK, 3 * CHUNK), 0), 1.0, 0.0).astype(BF)

    def chunk_rows(ci):
        return slice(ci * CHUNK, (ci + 1) * CHUNK)

    group_tables = [[] for _ in range(n_group)]
    states = [[st_s[j] for j in range(N_PAIR)]]

    def tables_stages(gi):
        chunks = []
        for ci in range(gi * CHUNK_GROUP, (gi + 1) * CHUNK_GROUP):
            rows = chunk_rows(ci)
            chunks.append((r_s[rows, :], k_s[rows, :], v_s[rows, :], a_s[rows, :], b_s[rows, :],
                           lw_s[rows, :]))
        yield from _wkv_tables_stages(chunks, tril, group_tables[gi])

    def apply_stages(gi):
        for i in range(CHUNK_GROUP):
            rows = chunk_rows(gi * CHUNK_GROUP + i)
            tables, e_tot = group_tables[gi][i]
            res = []
            yield from _wkv_apply_stages(tables, v_s[rows, :], e_tot, states[0], res)
            y_s[rows, :] = res[0][0]
            states[0] = res[0][1]

    _interleave(tables_stages(0), branch_a_out_stages())
    for gi in range(n_group):
        stages = [apply_stages(gi), gate_b_stages(gi)]
        if gi + 1 < n_group:
            stages.insert(0, tables_stages(gi + 1))
        if gi > 0:
            stages.append(post_stages(gi - 1))
        _interleave(*stages)
    for j in range(N_PAIR):
        st_s[j] = states[0][j]
    _interleave(post_stages(n_group - 1))


def _mixer(x, n_batch, vecs, win, wouta, lora, woutb, wo):
    m = x.shape[0]
    tiles = m // n_batch // TM_MIX
    consts = [vecs, win, wouta, lora, woutb, wo]
    tok_spec = pl.BlockSpec((TM_MIX, D_MODEL), lambda bi, ti: (bi * tiles + ti, 0))
    vec = lambda n: pltpu.VMEM((TM_MIX, n), F32)
    return pl.pallas_call(
        _mixer_body,
        grid=(n_batch, tiles),
        in_specs=[tok_spec] + [_const_spec(c.shape) for c in consts],
        out_specs=tok_spec,
        out_shape=jax.ShapeDtypeStruct((m, D_MODEL), F32),
        scratch_shapes=[
            pltpu.VMEM((TM_MIX + HALO, D_CONV), F32),
            pltpu.VMEM((TM_MIX + HALO, COLS_B), F32),
            vec(D_RWKV), vec(D_RWKV), vec(D_RWKV), vec(D_RWKV), vec(D_RWKV), vec(D_RWKV),
            vec(D_RWKV), vec(D_RWKV), vec(D_RWKV),
            vec(D_MODEL),
            pltpu.VMEM((N_PAIR, PAIR, PAIR), F32),
        ],
        compiler_params=pltpu.CompilerParams(
            dimension_semantics=("arbitrary", "arbitrary"), vmem_limit_bytes=VMEM_LIMIT),
        name="mixer",
    )(x, *consts)


def kernel(x, ffn1_norm, ffn1_w_gate, ffn1_w_up, ffn1_w_down, mix_norm, w_in, conv_w, w_out_a, mu_b, w0, w_decay_up, a0, w_iclr_up, w_gate_up, k_k, k_a, r_k, ln_x_w, ln_x_b, w_out_b, w_o, ffn2_norm, ffn2_w_gate, ffn2_w_up, ffn2_w_down, final_norm):
    n_batch, seq, d = x.shape
    assert d == D_MODEL and seq % TM_MIX == 0 and (n_batch * seq) % TM_FFN == 0
    assert ffn1_norm.shape[0] == 1, "single layer"
    row = lambda t: t.reshape(1, -1).astype(F32)
    xf = x.reshape(n_batch * seq, d)

    x1, w_in_bf = _ffn(xf, row(ffn1_norm[0]), ffn1_w_gate[0], ffn1_w_up[0], ffn1_w_down[0],
                       cast_to_bf16=w_in[0])

    rows = [mu_b[0], mix_norm[0], w0[0], a0[0], k_k[0], k_a[0], r_k[0], ln_x_w[0], ln_x_b[0],
            conv_w[0][0], conv_w[0][1], conv_w[0][2]]
    pieces = []
    for t in rows:
        t = t.reshape(-1).astype(F32)
        pieces += [t, jnp.zeros((COLS_B - t.size,), F32)]
    pieces.append(jnp.zeros(((VEC_ROWS - len(rows)) * COLS_B,), F32))
    vecs = jnp.concatenate(pieces).reshape(VEC_ROWS, COLS_B)
    zeros_lora = jnp.zeros((64, D_RWKV), F32)
    lora = jnp.concatenate([w_decay_up[0], zeros_lora, zeros_lora, w_iclr_up[0], w_gate_up[0]], axis=0)
    x2 = _mixer(x1, n_batch, vecs, w_in_bf, w_out_a[0], lora, w_out_b[0], w_o[0])

    out = _ffn(x2, row(ffn2_norm[0]), ffn2_w_gate[0], ffn2_w_up[0], ffn2_w_down[0],
               final_gain=row(final_norm))
    return out.reshape(n_batch, seq, d)
```

```python
import functools

import jax
import jax.numpy as jnp
from jax import lax
from jax.experimental import pallas as pl
from jax.experimental.pallas import tpu as pltpu

F32 = jnp.float32
BF = jnp.bfloat16

D_MODEL = 1024
D_CONV = 512
D_RWKV = 512
HEAD = 64
D_FF = 2816
COLS_A = 3 * D_CONV
COLS_B = 3 * D_RWKV + 64 + 64 + 128
OFF_B = COLS_A
OFF_GA = COLS_A + COLS_B
OFF_GB = OFF_GA + D_MODEL
RMS_EPS = 1e-6
GN_EPS = 64e-5
DECAY_SCALE = 0.6065306597126334

CHUNK = 64
PAIR = 2 * HEAD
N_PAIR = D_RWKV // PAIR
HALO = 8
TM_FFN = 512
TF_FFN = 256
TM_MIX = 512
CHUNK_GROUP = 4
GATE_COLS = 256
VMEM_LIMIT = 56 * 1024 * 1024


def _dot(a, b):
    return jnp.dot(a.astype(BF), b.astype(BF), preferred_element_type=F32)


def _dot_nt(a, b):
    return lax.dot_general(a.astype(BF), b.astype(BF), (((1,), (1,)), ((), ())),
                           preferred_element_type=F32)


def _dot_f32(a, b):
    return jnp.dot(a, b, preferred_element_type=F32)


def _split2(x):
    hi = x.astype(BF)
    lo = (x - hi.astype(F32)).astype(BF)
    return hi, lo


def _dot3(a, b):
    ah, al = _split2(a)
    bh, bl = _split2(b)
    return (jnp.dot(jnp.concatenate([ah, al], axis=1), jnp.concatenate([bh, bh], axis=0),
                    preferred_element_type=F32)
            + jnp.dot(ah, bl, preferred_element_type=F32))


def _rms_norm(x, gain):
    return x * lax.rsqrt(jnp.mean(x * x, axis=-1, keepdims=True) + RMS_EPS) * gain


def _ffn_body(*refs, final_norm, with_cast):
    refs = list(refs)
    x_ref, gain_ref, wg_ref, wu_ref, wd_ref = refs[:5]
    rest = refs[5:]
    fn_ref = rest.pop(0) if final_norm else None
    cast_in_ref = rest.pop(0) if with_cast else None
    o_ref = rest.pop(0)
    cast_out_ref = rest.pop(0) if with_cast else None
    act_ref, = rest
    if with_cast:
        cast_out_ref[...] = cast_in_ref[...].astype(BF)
    x = x_ref[...]
    h = _rms_norm(x, gain_ref[...])
    for c in range(D_FF // TF_FFN):
        sl = slice(c * TF_FFN, (c + 1) * TF_FFN)
        g = _dot_f32(h, wg_ref[:, sl])
        u = _dot_f32(h, wu_ref[:, sl])
        act_ref[:, sl] = g * jax.nn.sigmoid(g) * u
    y = x + 0.5 * _dot_f32(act_ref[...], wd_ref[...])
    if final_norm:
        y = _rms_norm(y, fn_ref[...])
    o_ref[...] = y


def _const_spec(shape):
    return pl.BlockSpec(shape, lambda *_: (0,) * len(shape), pipeline_mode=pl.Buffered(1))


def _ffn(x, gain, wg, wu, wd, final_gain=None, cast_to_bf16=None):
    m = x.shape[0]
    steps = m // TM_FFN
    final_norm = final_gain is not None
    with_cast = cast_to_bf16 is not None
    tok_spec = pl.BlockSpec((TM_FFN, D_MODEL), lambda i: (i, 0))
    in_specs = [tok_spec, _const_spec((1, D_MODEL)), _const_spec((D_MODEL, D_FF)),
                _const_spec((D_MODEL, D_FF)), _const_spec((D_FF, D_MODEL))]
    args = [x, gain, wg, wu, wd]
    out_specs = tok_spec
    out_shape = jax.ShapeDtypeStruct((m, D_MODEL), F32)
    if final_norm:
        in_specs.append(_const_spec((1, D_MODEL)))
        args.append(final_gain)
    if with_cast:
        rows, cols = cast_to_bf16.shape
        assert rows % steps == 0
        cast_spec = pl.BlockSpec((rows // steps, cols), lambda i: (i, 0))
        in_specs.append(cast_spec)
        args.append(cast_to_bf16)
        out_specs = (tok_spec, cast_spec)
        out_shape = (out_shape, jax.ShapeDtypeStruct((rows, cols), BF))
    return pl.pallas_call(
        functools.partial(_ffn_body, final_norm=final_norm, with_cast=with_cast),
        grid=(steps,),
        in_specs=in_specs,
        out_specs=out_specs,
        out_shape=out_shape,
        scratch_shapes=[pltpu.VMEM((TM_FFN, D_FF), F32)],
        compiler_params=pltpu.CompilerParams(
            dimension_semantics=("arbitrary",), vmem_limit_bytes=VMEM_LIMIT),
        name="ffn_final" if final_norm else "ffn",
    )(*args)


def _pair_index():
    row = lax.broadcasted_iota(jnp.int32, (PAIR, PAIR), 0)
    lane = lax.broadcasted_iota(jnp.int32, (PAIR, PAIR), 1)
    return row, lane


def _interleave(*gens):
    live = list(gens)
    while live:
        for g in list(live):
            try:
                next(g)
            except StopIteration:
                live.remove(g)


def _tri_inverse_stages(l_bds, out):
    row, lane = _pair_index()
    zero = jnp.zeros((PAIR, PAIR), F32)
    eye = jnp.where(row == lane, 1.0, 0.0).astype(F32)

    def off(m):
        return ((row // (2 * m)) == (lane // (2 * m))) & ((row % (2 * m)) >= m) & ((lane % (2 * m)) < m)

    ts = [eye + jnp.where(off(1), l, zero) for l in l_bds]
    m = 2
    while m < CHUNK:
        mask = off(m)
        tl = [_dot(t, jnp.where(mask, l, zero)) for t, l in zip(ts, l_bds)]
        yield
        ts = [t + _dot(x, t) for t, x in zip(ts, tl)]
        yield
        m *= 2
    out.extend(ts)


def _wkv_tables_stages(chunks, tril, out):
    c = CHUNK
    head0 = lax.broadcasted_iota(jnp.int32, (c, PAIR), 1) < HEAD
    row, lane = _pair_index()
    t_idx = row % c
    s_idx = lane % c
    blockdiag = (row // c) == (lane // c)
    zero = jnp.zeros((c, PAIR), F32)
    zero2 = jnp.zeros((PAIR, PAIR), F32)

    e_tots, lhs0, lhs1, rhs0, rhs1, at_bd, rt_bd, bkh, v_swap = [], [], [], [], [], [], [], [], []
    for r, k, v, a, b, lw in chunks:
        h1 = lw.astype(BF)
        r1 = lw - h1.astype(F32)
        h2 = r1.astype(BF)
        h3 = (r1 - h2.astype(F32)).astype(BF)
        cs = jnp.dot(tril, jnp.concatenate([h1, h2, h3], axis=0), preferred_element_type=F32)
        tot = cs[c - 1:c, :]
        e_tots.append(jnp.exp(tot))
        at = a * jnp.exp(cs - lw)
        rt = r * jnp.exp(cs)
        e_n = jnp.exp(-cs)
        bt = b * e_n
        kt = k * e_n
        bh = bt * e_tots[-1]
        kh = kt * e_tots[-1]
        for j in range(N_PAIR):
            sl = slice(j * PAIR, (j + 1) * PAIR)
            ar0 = jnp.concatenate([jnp.where(head0, at[:, sl], zero), jnp.where(head0, rt[:, sl], zero)], axis=0)
            ar1 = jnp.concatenate([jnp.where(head0, zero, at[:, sl]), jnp.where(head0, zero, rt[:, sl])], axis=0)
            lhs0.append(ar0)
            lhs1.append(ar1)
            rhs0.append(jnp.concatenate([bt[:, sl], kt[:, sl]], axis=0))
            rhs1.append(jnp.concatenate([kt[:, sl], bt[:, sl]], axis=0))
            at_bd.append(jnp.concatenate([ar0[:c], ar1[:c]], axis=0))
            rt_bd.append(jnp.concatenate([ar0[c:], ar1[c:]], axis=0))
            bkh.append(jnp.concatenate([bh[:, sl], kh[:, sl]], axis=0))
            v_swap.append(jnp.concatenate([jnp.where(head0, zero, v[:, sl]),
                                           jnp.where(head0, v[:, sl], zero)], axis=0))
    yield
    g0 = [_dot_nt(x, y) for x, y in zip(lhs0, rhs0)]
    g1 = [_dot_nt(x, y) for x, y in zip(lhs1, rhs1)]
    yield
    aa = [jnp.where(s_idx < t_idx, jnp.concatenate([x[:c], y[:c]], axis=0), zero2) for x, y in zip(g0, g1)]
    arr = [jnp.where(s_idx <= t_idx, jnp.concatenate([x[c:], y[c:]], axis=0), zero2) for x, y in zip(g0, g1)]
    l_bd = [jnp.where(blockdiag, x, zero2) for x in aa]
    ak_ad = [jnp.where(blockdiag, zero2, x) for x in aa]
    akv = [_dot(x, y) for x, y in zip(ak_ad, v_swap)]
    yield
    t_inv = []
    yield from _tri_inverse_stages(l_bd, t_inv)
    t_at_akv = [_dot(t, jnp.concatenate([x, w], axis=1)) for t, x, w in zip(t_inv, at_bd, akv)]
    yield
    tar = [jnp.concatenate([z[:, :PAIR], y], axis=0) for z, y in zip(t_at_akv, rt_bd)]
    u_v = [z[:, PAIR:] for z in t_at_akv]
    for i in range(len(chunks)):
        sl = slice(i * N_PAIR, (i + 1) * N_PAIR)
        out.append(((tar[sl], u_v[sl], arr[sl], bkh[sl], v_swap[sl]), e_tots[i]))


def _wkv_apply_stages(tables, v, e_tot, states, out):
    c = CHUNK
    tar, u_v, arr, bkh, v_swap = tables
    head0 = lax.broadcasted_iota(jnp.int32, (c, PAIR), 1) < HEAD
    row, lane = _pair_index()
    blockdiag = (row // c) == (lane // c)
    zero2 = jnp.zeros((PAIR, PAIR), F32)
    xs = [_dot_nt(x, s) for x, s in zip(tar, states)]
    yield
    u_bd = [z[:PAIR] + w for z, w in zip(xs, u_v)]
    rs_bd = [z[PAIR:] for z in xs]
    y_bd = [_dot(m, u + w) + z for m, u, w, z in zip(arr, u_bd, v_swap, rs_bd)]
    ys = [jnp.where(head0, z[:c], z[c:]) for z in y_bd]
    new_states = []
    for j in range(N_PAIR):
        sl = slice(j * PAIR, (j + 1) * PAIR)
        uv = jnp.concatenate([u_bd[j][:c] + u_bd[j][c:], v[:, sl]], axis=0)
        upd = _dot(uv.T, bkh[j])
        new_states.append(jnp.where(blockdiag, states[j] * e_tot[:, sl] + upd, zero2))
    yield
    out.append((jnp.concatenate(ys, axis=1), new_states))


def _mixer_body(x_ref, gain_ref, win_ref, convw_ref, wouta_ref, mu_ref, w0_ref, wdec_ref,
                a0_ref, wicl_ref, wgate_ref, kk_ref, ka_ref, rk_ref, lnw_ref, lnb_ref,
                woutb_ref, wo_ref,
                o_ref,
                cu_buf, pb_buf, r_s, k_s, v_s, a_s, b_s, lw_s, g_s, bon_s, y_s, m_s, st_s):
    tm = TM_MIX

    @pl.when(pl.program_id(1) == 0)
    def _():
        cu_buf[0:HALO, :] = jnp.zeros((HALO, D_CONV), F32)
        pb_buf[0:HALO, :] = jnp.zeros((HALO, COLS_B), F32)
        st_s[...] = jnp.zeros_like(st_s)

    x = x_ref[...]
    h = _rms_norm(x, gain_ref[...]).astype(BF)

    def head_sum(z):
        n = z.shape[0]
        first = lax.broadcasted_iota(jnp.int32, (n, PAIR), 1) < HEAD
        outs = []
        for j in range(N_PAIR):
            t = z[:, j * PAIR:(j + 1) * PAIR]
            s0 = jnp.sum(jnp.where(first, t, 0.0), axis=-1, keepdims=True)
            s1 = jnp.sum(jnp.where(first, 0.0, t), axis=-1, keepdims=True)
            outs.append(jnp.where(first, s0, s1))
        return jnp.concatenate(outs, axis=1)

    def proj(lo, hi):
        return jnp.dot(h, win_ref[:, lo:hi], preferred_element_type=F32)

    pa_parts = []

    def branch_a_proj_stages():
        for p in range(COLS_A // D_CONV):
            pa_parts.append(proj(p * D_CONV, (p + 1) * D_CONV))
            yield

    def branch_b_prep_stages():
        pb = proj(OFF_B, OFF_GA)
        yield
        pb_buf[HALO:HALO + tm, :] = pb
        prev = pb_buf[pl.ds(HALO - 1, tm), :]
        pb_buf[0:HALO, :] = pb_buf[tm:tm + HALO, :]
        pbm = pb + (prev - pb) * mu_ref[...]
        r = pbm[:, 0:D_RWKV]
        k = pbm[:, D_RWKV:2 * D_RWKV]
        v = pbm[:, 2 * D_RWKV:3 * D_RWKV]
        xwa = pbm[:, 3 * D_RWKV:3 * D_RWKV + 128]
        xg = pbm[:, 3 * D_RWKV + 128:COLS_B]
        zeros_lora = jnp.zeros_like(wdec_ref[...])
        wdec = jnp.concatenate([wdec_ref[...], zeros_lora], axis=0)
        wicl = jnp.concatenate([zeros_lora, wicl_ref[...]], axis=0)
        z = w0_ref[...] + _dot3(jnp.tanh(xwa), wdec)
        lw_s[...] = -DECAY_SCALE * jax.nn.sigmoid(z)
        iclr = jax.nn.sigmoid(a0_ref[...] + _dot3(xwa, wicl))
        g_s[...] = _dot_f32(jax.nn.sigmoid(xg), wgate_ref[...])
        yield
        kk = k * kk_ref[...]
        kk = kk * lax.rsqrt(jnp.maximum(head_sum(kk * kk), 1e-24))
        k2 = k * (1.0 + (iclr - 1.0) * ka_ref[...])
        r_s[...] = r
        k_s[...] = k2
        v_s[...] = v
        a_s[...] = -kk
        b_s[...] = kk * iclr
        yield
        bon_s[...] = head_sum(r * k2 * rk_ref[...]) * v
        yield

    _interleave(branch_b_prep_stages(), branch_a_proj_stages())

    cu = pa_parts[1] * pa_parts[2]
    cu_buf[HALO:HALO + tm, :] = cu
    cw = convw_ref[...]
    conv = (cw[2:3] * cu + cw[1:2] * cu_buf[pl.ds(HALO - 1, tm), :]
            + cw[0:1] * cu_buf[pl.ds(HALO - 2, tm), :])
    cu_buf[0:HALO, :] = cu_buf[tm:tm + HALO, :]
    gated = pa_parts[0] * conv

    def branch_a_out_stages():
        for q in range(D_MODEL // GATE_COLS):
            cols = slice(q * GATE_COLS, (q + 1) * GATE_COLS)
            ya = _dot_f32(gated, wouta_ref[:, cols])
            ga = proj(OFF_GA + q * GATE_COLS, OFF_GA + (q + 1) * GATE_COLS)
            m_s[:, cols] = jax.nn.sigmoid(ga) * ya
            yield

    group_rows = CHUNK_GROUP * CHUNK
    n_group = tm // group_rows
    sgb_parts = [[] for _ in range(n_group)]

    def gate_b_stages(gi):
        rows = slice(gi * group_rows, (gi + 1) * group_rows)
        for q in range(D_MODEL // GATE_COLS):
            lo = OFF_GB + q * GATE_COLS
            gb = jnp.dot(h[rows], win_ref[:, lo:lo + GATE_COLS], preferred_element_type=F32)
            sgb_parts[gi].append(jax.nn.sigmoid(gb))
            yield

    def post_stages(gi):
        rows = slice(gi * group_rows, (gi + 1) * group_rows)
        y = y_s[rows, :]
        mean = head_sum(y) * (1.0 / HEAD)
        yield
        d = y - mean
        var = head_sum(d * d) * (1.0 / HEAD)
        yield
        yn = d * lax.rsqrt(var + GN_EPS) * lnw_ref[...] + lnb_ref[...] + bon_s[rows, :]
        yb = _dot_f32(yn * g_s[rows, :], woutb_ref[...])
        yield
        merged = m_s[rows, :] + jnp.concatenate(sgb_parts[gi], axis=1) * yb
        o_ref[rows, :] = x[rows] + _dot_f32(merged, wo_ref[...])
        yield

    tril = jnp.where(lax.broadcasted_iota(jnp.int32, (CHUNK, 3 * CHUNK), 1) % CHUNK
                     <= lax.broadcasted_iota(jnp.int32, (CHUNK, 3 * CHUNK), 0), 1.0, 0.0).astype(BF)

    def chunk_rows(ci):
        return slice(ci * CHUNK, (ci + 1) * CHUNK)

    group_tables = [[] for _ in range(n_group)]
    states = [[st_s[j] for j in range(N_PAIR)]]

    def tables_stages(gi):
        chunks = []
        for ci in range(gi * CHUNK_GROUP, (gi + 1) * CHUNK_GROUP):
            rows = chunk_rows(ci)
            chunks.append((r_s[rows, :], k_s[rows, :], v_s[rows, :], a_s[rows, :], b_s[rows, :],
                           lw_s[rows, :]))
        yield from _wkv_tables_stages(chunks, tril, group_tables[gi])

    def apply_stages(gi):
        for i in range(CHUNK_GROUP):
            rows = chunk_rows(gi * CHUNK_GROUP + i)
            tables, e_tot = group_tables[gi][i]
            res = []
            yield from _wkv_apply_stages(tables, v_s[rows, :], e_tot, states[0], res)
            y_s[rows, :] = res[0][0]
            states[0] = res[0][1]

    _interleave(tables_stages(0), branch_a_out_stages())
    for gi in range(n_group):
        stages = [apply_stages(gi), gate_b_stages(gi)]
        if gi + 1 < n_group:
            stages.insert(0, tables_stages(gi + 1))
        if gi > 0:
            stages.append(post_stages(gi - 1))
        _interleave(*stages)
    for j in range(N_PAIR):
        st_s[j] = states[0][j]
    _interleave(post_stages(n_group - 1))


def _mixer(x, n_batch, *consts):
    m = x.shape[0]
    tiles = m // n_batch // TM_MIX
    tok_spec = pl.BlockSpec((TM_MIX, D_MODEL), lambda bi, ti: (bi * tiles + ti, 0))
    vec = lambda n: pltpu.VMEM((TM_MIX, n), F32)
    return pl.pallas_call(
        _mixer_body,
        grid=(n_batch, tiles),
        in_specs=[tok_spec] + [_const_spec(c.shape) for c in consts],
        out_specs=tok_spec,
        out_shape=jax.ShapeDtypeStruct((m, D_MODEL), F32),
        scratch_shapes=[
            pltpu.VMEM((TM_MIX + HALO, D_CONV), F32),
            pltpu.VMEM((TM_MIX + HALO, COLS_B), F32),
            vec(D_RWKV), vec(D_RWKV), vec(D_RWKV), vec(D_RWKV), vec(D_RWKV), vec(D_RWKV),
            vec(D_RWKV), vec(D_RWKV), vec(D_RWKV),
            vec(D_MODEL),
            pltpu.VMEM((N_PAIR, PAIR, PAIR), F32),
        ],
        compiler_params=pltpu.CompilerParams(
            dimension_semantics=("arbitrary", "arbitrary"), vmem_limit_bytes=VMEM_LIMIT),
        name="mixer",
    )(x, *consts)


def kernel(x, ffn1_norm, ffn1_w_gate, ffn1_w_up, ffn1_w_down, mix_norm, w_in, conv_w, w_out_a, mu_b, w0, w_decay_up, a0, w_iclr_up, w_gate_up, k_k, k_a, r_k, ln_x_w, ln_x_b, w_out_b, w_o, ffn2_norm, ffn2_w_gate, ffn2_w_up, ffn2_w_down, final_norm):
    n_batch, seq, d = x.shape
    assert d == D_MODEL and seq % TM_MIX == 0 and (n_batch * seq) % TM_FFN == 0
    assert ffn1_norm.shape[0] == 1, "single layer"
    row = lambda t: t.reshape(1, -1).astype(F32)
    xf = x.reshape(n_batch * seq, d)

    x1, w_in_bf = _ffn(xf, row(ffn1_norm[0]), ffn1_w_gate[0], ffn1_w_up[0], ffn1_w_down[0],
                       cast_to_bf16=w_in[0])

    x2 = _mixer(x1, n_batch, mix_norm, w_in_bf, conv_w[0], w_out_a[0], mu_b, w0, w_decay_up[0], a0,
                w_iclr_up[0], w_gate_up[0], k_k, k_a, r_k.reshape(1, D_RWKV), ln_x_w, ln_x_b,
                w_out_b[0], w_o[0])

    out = _ffn(x2, row(ffn2_norm[0]), ffn2_w_gate[0], ffn2_w_up[0], ffn2_w_down[0],
               final_gain=row(final_norm))
    return out.reshape(n_batch, seq, d)
```

```python
import functools

import jax
import jax.numpy as jnp
from jax import lax
from jax.experimental import pallas as pl
from jax.experimental.pallas import tpu as pltpu

F32 = jnp.float32
BF = jnp.bfloat16

D_MODEL = 1024
D_CONV = 512
D_RWKV = 512
HEAD = 64
D_FF = 2816
COLS_A = 3 * D_CONV
COLS_B = 3 * D_RWKV + 64 + 64 + 128
OFF_B = COLS_A
OFF_GA = COLS_A + COLS_B
OFF_GB = OFF_GA + D_MODEL
RMS_EPS = 1e-6
GN_EPS = 64e-5
DECAY_SCALE = 0.6065306597126334

CHUNK = 64
PAIR = 2 * HEAD
N_PAIR = D_RWKV // PAIR
HALO = 8
TM_FFN = 512
TF_FFN = 256
FFN_WEIGHT_GROUPS = ((0, 1), (1, 4), (4, 8), (8, 11))
TM_MIX = 512
CHUNK_GROUP = 4
GATE_COLS = 256
VMEM_LIMIT = 56 * 1024 * 1024


def _dot(a, b):
    return jnp.dot(a.astype(BF), b.astype(BF), preferred_element_type=F32)


def _dot_nt(a, b):
    return lax.dot_general(a.astype(BF), b.astype(BF), (((1,), (1,)), ((), ())),
                           preferred_element_type=F32)


def _dot_f32(a, b):
    return jnp.dot(a, b, preferred_element_type=F32)


def _split2(x):
    hi = x.astype(BF)
    lo = (x - hi.astype(F32)).astype(BF)
    return hi, lo


def _dot3(a, b):
    ah, al = _split2(a)
    bh, bl = _split2(b)
    return (jnp.dot(jnp.concatenate([ah, al], axis=1), jnp.concatenate([bh, bh], axis=0),
                    preferred_element_type=F32)
            + jnp.dot(ah, bl, preferred_element_type=F32))


def _rms_norm(x, gain):
    return x * lax.rsqrt(jnp.mean(x * x, axis=-1, keepdims=True) + RMS_EPS) * gain


def _ffn_body(*refs, final_norm, with_cast):
    refs = list(refs)
    x_ref, gain_ref, wg_hbm, wu_hbm, wd_hbm = refs[:5]
    rest = refs[5:]
    fn_ref = rest.pop(0) if final_norm else None
    cast_in_ref = rest.pop(0) if with_cast else None
    o_ref = rest.pop(0)
    cast_out_ref = rest.pop(0) if with_cast else None
    wg_ref, wu_ref, wd_ref, act_ref, sems = rest

    def group_copies(gi):
        lo, hi = FFN_WEIGHT_GROUPS[gi]
        cols = pl.ds(lo * TF_FFN, (hi - lo) * TF_FFN)
        return (pltpu.make_async_copy(wg_hbm.at[:, cols], wg_ref.at[:, cols], sems.at[0, gi]),
                pltpu.make_async_copy(wu_hbm.at[:, cols], wu_ref.at[:, cols], sems.at[1, gi]))

    down_copy = pltpu.make_async_copy(wd_hbm, wd_ref, sems.at[2, 0])

    def compute(first_step):
        if with_cast:
            cast_out_ref[...] = cast_in_ref[...].astype(BF)
        x = x_ref[...]
        h = _rms_norm(x, gain_ref[...])
        for c in range(D_FF // TF_FFN):
            if first_step:
                for gi, (lo, _) in enumerate(FFN_WEIGHT_GROUPS):
                    if c == lo:
                        for copy in group_copies(gi):
                            copy.wait()
            sl = slice(c * TF_FFN, (c + 1) * TF_FFN)
            g = _dot_f32(h, wg_ref[:, sl])
            u = _dot_f32(h, wu_ref[:, sl])
            act_ref[:, sl] = g * jax.nn.sigmoid(g) * u
        if first_step:
            down_copy.wait()
        y = x + 0.5 * _dot_f32(act_ref[...], wd_ref[...])
        if final_norm:
            y = _rms_norm(y, fn_ref[...])
        o_ref[...] = y

    step = pl.program_id(0)

    @pl.when(step == 0)
    def _():
        for gi in range(len(FFN_WEIGHT_GROUPS)):
            for copy in group_copies(gi):
                copy.start()
        down_copy.start()
        compute(True)

    @pl.when(step != 0)
    def _():
        compute(False)


def _const_spec(shape):
    return pl.BlockSpec(shape, lambda *_: (0,) * len(shape), pipeline_mode=pl.Buffered(1))


def _ffn(x, gain, wg, wu, wd, final_gain=None, cast_to_bf16=None):
    m = x.shape[0]
    steps = m // TM_FFN
    final_norm = final_gain is not None
    with_cast = cast_to_bf16 is not None
    tok_spec = pl.BlockSpec((TM_FFN, D_MODEL), lambda i: (i, 0))
    hbm_spec = pl.BlockSpec(memory_space=pl.ANY)
    in_specs = [tok_spec, _const_spec((1, D_MODEL)), hbm_spec, hbm_spec, hbm_spec]
    args = [x, gain, wg, wu, wd]
    out_specs = tok_spec
    out_shape = jax.ShapeDtypeStruct((m, D_MODEL), F32)
    if final_norm:
        in_specs.append(_const_spec((1, D_MODEL)))
        args.append(final_gain)
    if with_cast:
        rows, cols = cast_to_bf16.shape
        assert rows % steps == 0
        cast_spec = pl.BlockSpec((rows // steps, cols), lambda i: (i, 0))
        in_specs.append(cast_spec)
        args.append(cast_to_bf16)
        out_specs = (tok_spec, cast_spec)
        out_shape = (out_shape, jax.ShapeDtypeStruct((rows, cols), BF))
    return pl.pallas_call(
        functools.partial(_ffn_body, final_norm=final_norm, with_cast=with_cast),
        grid=(steps,),
        in_specs=in_specs,
        out_specs=out_specs,
        out_shape=out_shape,
        scratch_shapes=[
            pltpu.VMEM((D_MODEL, D_FF), F32),
            pltpu.VMEM((D_MODEL, D_FF), F32),
            pltpu.VMEM((D_FF, D_MODEL), F32),
            pltpu.VMEM((TM_FFN, D_FF), F32),
            pltpu.SemaphoreType.DMA((3, len(FFN_WEIGHT_GROUPS))),
        ],
        compiler_params=pltpu.CompilerParams(
            dimension_semantics=("arbitrary",), vmem_limit_bytes=VMEM_LIMIT),
        name="ffn_final" if final_norm else "ffn",
    )(*args)


def _pair_index():
    row = lax.broadcasted_iota(jnp.int32, (PAIR, PAIR), 0)
    lane = lax.broadcasted_iota(jnp.int32, (PAIR, PAIR), 1)
    return row, lane


def _interleave(*gens):
    live = list(gens)
    while live:
        for g in list(live):
            try:
                next(g)
            except StopIteration:
                live.remove(g)


def _tri_inverse_stages(l_bds, out):
    row, lane = _pair_index()
    zero = jnp.zeros((PAIR, PAIR), F32)
    eye = jnp.where(row == lane, 1.0, 0.0).astype(F32)

    def off(m):
        return ((row // (2 * m)) == (lane // (2 * m))) & ((row % (2 * m)) >= m) & ((lane % (2 * m)) < m)

    ts = [eye + jnp.where(off(1), l, zero) for l in l_bds]
    m = 2
    while m < CHUNK:
        mask = off(m)
        tl = [_dot(t, jnp.where(mask, l, zero)) for t, l in zip(ts, l_bds)]
        yield
        ts = [t + _dot(x, t) for t, x in zip(ts, tl)]
        yield
        m *= 2
    out.extend(ts)


def _wkv_tables_stages(chunks, tril, out):
    c = CHUNK
    head0 = lax.broadcasted_iota(jnp.int32, (c, PAIR), 1) < HEAD
    row, lane = _pair_index()
    t_idx = row % c
    s_idx = lane % c
    blockdiag = (row // c) == (lane // c)
    zero = jnp.zeros((c, PAIR), F32)
    zero2 = jnp.zeros((PAIR, PAIR), F32)

    e_tots, lhs0, lhs1, rhs0, rhs1, at_bd, rt_bd, bkh, v_swap = [], [], [], [], [], [], [], [], []
    for r, k, v, a, b, lw in chunks:
        h1 = lw.astype(BF)
        r1 = lw - h1.astype(F32)
        h2 = r1.astype(BF)
        h3 = (r1 - h2.astype(F32)).astype(BF)
        cs = jnp.dot(tril, jnp.concatenate([h1, h2, h3], axis=0), preferred_element_type=F32)
        tot = cs[c - 1:c, :]
        e_tots.append(jnp.exp(tot))
        at = a * jnp.exp(cs - lw)
        rt = r * jnp.exp(cs)
        e_n = jnp.exp(-cs)
        bt = b * e_n
        kt = k * e_n
        bh = bt * e_tots[-1]
        kh = kt * e_tots[-1]
        for j in range(N_PAIR):
            sl = slice(j * PAIR, (j + 1) * PAIR)
            ar0 = jnp.concatenate([jnp.where(head0, at[:, sl], zero), jnp.where(head0, rt[:, sl], zero)], axis=0)
            ar1 = jnp.concatenate([jnp.where(head0, zero, at[:, sl]), jnp.where(head0, zero, rt[:, sl])], axis=0)
            lhs0.append(ar0)
            lhs1.append(ar1)
            rhs0.append(jnp.concatenate([bt[:, sl], kt[:, sl]], axis=0))
            rhs1.append(jnp.concatenate([kt[:, sl], bt[:, sl]], axis=0))
            at_bd.append(jnp.concatenate([ar0[:c], ar1[:c]], axis=0))
            rt_bd.append(jnp.concatenate([ar0[c:], ar1[c:]], axis=0))
            bkh.append(jnp.concatenate([bh[:, sl], kh[:, sl]], axis=0))
            v_swap.append(jnp.concatenate([jnp.where(head0, zero, v[:, sl]),
                                           jnp.where(head0, v[:, sl], zero)], axis=0))
    yield
    g0 = [_dot_nt(x, y) for x, y in zip(lhs0, rhs0)]
    g1 = [_dot_nt(x, y) for x, y in zip(lhs1, rhs1)]
    yield
    aa = [jnp.where(s_idx < t_idx, jnp.concatenate([x[:c], y[:c]], axis=0), zero2) for x, y in zip(g0, g1)]
    arr = [jnp.where(s_idx <= t_idx, jnp.concatenate([x[c:], y[c:]], axis=0), zero2) for x, y in zip(g0, g1)]
    l_bd = [jnp.where(blockdiag, x, zero2) for x in aa]
    ak_ad = [jnp.where(blockdiag, zero2, x) for x in aa]
    akv = [_dot(x, y) for x, y in zip(ak_ad, v_swap)]
    yield
    t_inv = []
    yield from _tri_inverse_stages(l_bd, t_inv)
    t_at_akv = [_dot(t, jnp.concatenate([x, w], axis=1)) for t, x, w in zip(t_inv, at_bd, akv)]
    yield
    tar = [jnp.concatenate([z[:, :PAIR], y], axis=0) for z, y in zip(t_at_akv, rt_bd)]
    u_v = [z[:, PAIR:] for z in t_at_akv]
    for i in range(len(chunks)):
        sl = slice(i * N_PAIR, (i + 1) * N_PAIR)
        out.append(((tar[sl], u_v[sl], arr[sl], bkh[sl], v_swap[sl]), e_tots[i]))


def _wkv_apply_stages(tables, v, e_tot, states, out):
    c = CHUNK
    tar, u_v, arr, bkh, v_swap = tables
    head0 = lax.broadcasted_iota(jnp.int32, (c, PAIR), 1) < HEAD
    row, lane = _pair_index()
    blockdiag = (row // c) == (lane // c)
    zero2 = jnp.zeros((PAIR, PAIR), F32)
    xs = [_dot_nt(x, s) for x, s in zip(tar, states)]
    yield
    u_bd = [z[:PAIR] + w for z, w in zip(xs, u_v)]
    rs_bd = [z[PAIR:] for z in xs]
    y_bd = [_dot(m, u + w) + z for m, u, w, z in zip(arr, u_bd, v_swap, rs_bd)]
    ys = [jnp.where(head0, z[:c], z[c:]) for z in y_bd]
    new_states = []
    for j in range(N_PAIR):
        sl = slice(j * PAIR, (j + 1) * PAIR)
        uv = jnp.concatenate([u_bd[j][:c] + u_bd[j][c:], v[:, sl]], axis=0)
        upd = _dot(uv.T, bkh[j])
        new_states.append(jnp.where(blockdiag, states[j] * e_tot[:, sl] + upd, zero2))
    yield
    out.append((jnp.concatenate(ys, axis=1), new_states))


def _mixer_body(x_ref, gain_ref, win_ref, convw_ref, wouta_ref, mu_ref, w0_ref, wdec_ref,
                a0_ref, wicl_ref, wgate_ref, kk_ref, ka_ref, rk_ref, lnw_ref, lnb_ref,
                woutb_ref, wo_ref,
                o_ref,
                cu_buf, pb_buf, r_s, k_s, v_s, a_s, b_s, lw_s, g_s, bon_s, y_s, m_s, st_s):
    tm = TM_MIX

    @pl.when(pl.program_id(1) == 0)
    def _():
        cu_buf[0:HALO, :] = jnp.zeros((HALO, D_CONV), F32)
        pb_buf[0:HALO, :] = jnp.zeros((HALO, COLS_B), F32)
        st_s[...] = jnp.zeros_like(st_s)

    x = x_ref[...]
    h = _rms_norm(x, gain_ref[...]).astype(BF)

    def head_sum(z):
        n = z.shape[0]
        first = lax.broadcasted_iota(jnp.int32, (n, PAIR), 1) < HEAD
        outs = []
        for j in range(N_PAIR):
            t = z[:, j * PAIR:(j + 1) * PAIR]
            s0 = jnp.sum(jnp.where(first, t, 0.0), axis=-1, keepdims=True)
            s1 = jnp.sum(jnp.where(first, 0.0, t), axis=-1, keepdims=True)
            outs.append(jnp.where(first, s0, s1))
        return jnp.concatenate(outs, axis=1)

    def proj(lo, hi):
        return jnp.dot(h, win_ref[:, lo:hi], preferred_element_type=F32)

    pa_parts = []

    def branch_a_proj_stages():
        for p in range(COLS_A // D_CONV):
            pa_parts.append(proj(p * D_CONV, (p + 1) * D_CONV))
            yield

    def branch_b_prep_stages():
        pb = proj(OFF_B, OFF_GA)
        yield
        pb_buf[HALO:HALO + tm, :] = pb
        prev = pb_buf[pl.ds(HALO - 1, tm), :]
        pb_buf[0:HALO, :] = pb_buf[tm:tm + HALO, :]
        pbm = pb + (prev - pb) * mu_ref[...]
        r = pbm[:, 0:D_RWKV]
        k = pbm[:, D_RWKV:2 * D_RWKV]
        v = pbm[:, 2 * D_RWKV:3 * D_RWKV]
        xwa = pbm[:, 3 * D_RWKV:3 * D_RWKV + 128]
        xg = pbm[:, 3 * D_RWKV + 128:COLS_B]
        zeros_lora = jnp.zeros_like(wdec_ref[...])
        wdec = jnp.concatenate([wdec_ref[...], zeros_lora], axis=0)
        wicl = jnp.concatenate([zeros_lora, wicl_ref[...]], axis=0)
        z = w0_ref[...] + _dot3(jnp.tanh(xwa), wdec)
        lw_s[...] = -DECAY_SCALE * jax.nn.sigmoid(z)
        iclr = jax.nn.sigmoid(a0_ref[...] + _dot3(xwa, wicl))
        g_s[...] = _dot_f32(jax.nn.sigmoid(xg), wgate_ref[...])
        yield
        kk = k * kk_ref[...]
        kk = kk * lax.rsqrt(jnp.maximum(head_sum(kk * kk), 1e-24))
        k2 = k * (1.0 + (iclr - 1.0) * ka_ref[...])
        r_s[...] = r
        k_s[...] = k2
        v_s[...] = v
        a_s[...] = -kk
        b_s[...] = kk * iclr
        yield
        bon_s[...] = head_sum(r * k2 * rk_ref[...]) * v
        yield

    _interleave(branch_b_prep_stages(), branch_a_proj_stages())

    cu = pa_parts[1] * pa_parts[2]
    cu_buf[HALO:HALO + tm, :] = cu
    cw = convw_ref[...]
    conv = (cw[2:3] * cu + cw[1:2] * cu_buf[pl.ds(HALO - 1, tm), :]
            + cw[0:1] * cu_buf[pl.ds(HALO - 2, tm), :])
    cu_buf[0:HALO, :] = cu_buf[tm:tm + HALO, :]
    gated = pa_parts[0] * conv

    def branch_a_out_stages():
        for q in range(D_MODEL // GATE_COLS):
            cols = slice(q * GATE_COLS, (q + 1) * GATE_COLS)
            ya = _dot_f32(gated, wouta_ref[:, cols])
            ga = proj(OFF_GA + q * GATE_COLS, OFF_GA + (q + 1) * GATE_COLS)
            m_s[:, cols] = jax.nn.sigmoid(ga) * ya
            yield

    group_rows = CHUNK_GROUP * CHUNK
    n_group = tm // group_rows
    sgb_parts = [[] for _ in range(n_group)]

    def gate_b_stages(gi):
        rows = slice(gi * group_rows, (gi + 1) * group_rows)
        for q in range(D_MODEL // GATE_COLS):
            lo = OFF_GB + q * GATE_COLS
            gb = jnp.dot(h[rows], win_ref[:, lo:lo + GATE_COLS], preferred_element_type=F32)
            sgb_parts[gi].append(jax.nn.sigmoid(gb))
            yield

    def post_stages(gi):
        rows = slice(gi * group_rows, (gi + 1) * group_rows)
        y = y_s[rows, :]
        mean = head_sum(y) * (1.0 / HEAD)
        yield
        d = y - mean
        var = head_sum(d * d) * (1.0 / HEAD)
        yield
        yn = d * lax.rsqrt(var + GN_EPS) * lnw_ref[...] + lnb_ref[...] + bon_s[rows, :]
        yb = _dot_f32(yn * g_s[rows, :], woutb_ref[...])
        yield
        merged = m_s[rows, :] + jnp.concatenate(sgb_parts[gi], axis=1) * yb
        o_ref[rows, :] = x[rows] + _dot_f32(merged, wo_ref[...])
        yield

    tril = jnp.where(lax.broadcasted_iota(jnp.int32, (CHUNK, 3 * CHUNK), 1) % CHUNK
                     <= lax.broadcasted_iota(jnp.int32, (CHUNK, 3 * CHUNK), 0), 1.0, 0.0).astype(BF)

    def chunk_rows(ci):
        return slice(ci * CHUNK, (ci + 1) * CHUNK)

    group_tables = [[] for _ in range(n_group)]
    states = [[st_s[j] for j in range(N_PAIR)]]

    def tables_stages(gi):
        chunks = []
        for ci in range(gi * CHUNK_GROUP, (gi + 1) * CHUNK_GROUP):
            rows = chunk_rows(ci)
            chunks.append((r_s[rows, :], k_s[rows, :], v_s[rows, :], a_s[rows, :], b_s[rows, :],
                           lw_s[rows, :]))
        yield from _wkv_tables_stages(chunks, tril, group_tables[gi])

    def apply_stages(gi):
        for i in range(CHUNK_GROUP):
            rows = chunk_rows(gi * CHUNK_GROUP + i)
            tables, e_tot = group_tables[gi][i]
            res = []
            yield from _wkv_apply_stages(tables, v_s[rows, :], e_tot, states[0], res)
            y_s[rows, :] = res[0][0]
            states[0] = res[0][1]

    _interleave(tables_stages(0), branch_a_out_stages())
    for gi in range(n_group):
        stages = [apply_stages(gi), gate_b_stages(gi)]
        if gi + 1 < n_group:
            stages.insert(0, tables_stages(gi + 1))
        if gi > 0:
            stages.append(post_stages(gi - 1))
        _interleave(*stages)
    for j in range(N_PAIR):
        st_s[j] = states[0][j]
    _interleave(post_stages(n_group - 1))


def _mixer(x, n_batch, *consts):
    m = x.shape[0]
    tiles = m // n_batch // TM_MIX
    tok_spec = pl.BlockSpec((TM_MIX, D_MODEL), lambda bi, ti: (bi * tiles + ti, 0))
    vec = lambda n: pltpu.VMEM((TM_MIX, n), F32)
    return pl.pallas_call(
        _mixer_body,
        grid=(n_batch, tiles),
        in_specs=[tok_spec] + [_const_spec(c.shape) for c in consts],
        out_specs=tok_spec,
        out_shape=jax.ShapeDtypeStruct((m, D_MODEL), F32),
        scratch_shapes=[
            pltpu.VMEM((TM_MIX + HALO, D_CONV), F32),
            pltpu.VMEM((TM_MIX + HALO, COLS_B), F32),
            vec(D_RWKV), vec(D_RWKV), vec(D_RWKV), vec(D_RWKV), vec(D_RWKV), vec(D_RWKV),
            vec(D_RWKV), vec(D_RWKV), vec(D_RWKV),
            vec(D_MODEL),
            pltpu.VMEM((N_PAIR, PAIR, PAIR), F32),
        ],
        compiler_params=pltpu.CompilerParams(
            dimension_semantics=("arbitrary", "arbitrary"), vmem_limit_bytes=VMEM_LIMIT),
        name="mixer",
    )(x, *consts)


def kernel(x, ffn1_norm, ffn1_w_gate, ffn1_w_up, ffn1_w_down, mix_norm, w_in, conv_w, w_out_a, mu_b, w0, w_decay_up, a0, w_iclr_up, w_gate_up, k_k, k_a, r_k, ln_x_w, ln_x_b, w_out_b, w_o, ffn2_norm, ffn2_w_gate, ffn2_w_up, ffn2_w_down, final_norm):
    n_batch, seq, d = x.shape
    assert d == D_MODEL and seq % TM_MIX == 0 and (n_batch * seq) % TM_FFN == 0
    assert ffn1_norm.shape[0] == 1, "single layer"
    row = lambda t: t.reshape(1, -1).astype(F32)
    xf = x.reshape(n_batch * seq, d)

    x1, w_in_bf = _ffn(xf, row(ffn1_norm[0]), ffn1_w_gate[0], ffn1_w_up[0], ffn1_w_down[0],
                       cast_to_bf16=w_in[0])

    x2 = _mixer(x1, n_batch, mix_norm, w_in_bf, conv_w[0], w_out_a[0], mu_b, w0, w_decay_up[0], a0,
                w_iclr_up[0], w_gate_up[0], k_k, k_a, r_k.reshape(1, D_RWKV), ln_x_w, ln_x_b,
                w_out_b[0], w_o[0])

    out = _ffn(x2, row(ffn2_norm[0]), ffn2_w_gate[0], ffn2_w_up[0], ffn2_w_down[0],
               final_gain=row(final_norm))
    return out.reshape(n_batch, seq, d)
```

```python
import functools

import jax
import jax.numpy as jnp
from jax import lax
from jax.experimental import pallas as pl
from jax.experimental.pallas import tpu as pltpu

F32 = jnp.float32
BF = jnp.bfloat16

D_MODEL = 1024
D_CONV = 512
D_RWKV = 512
HEAD = 64
D_FF = 2816
COLS_A = 3 * D_CONV
COLS_B = 3 * D_RWKV + 64 + 64 + 128
OFF_B = COLS_A
OFF_GA = COLS_A + COLS_B
OFF_GB = OFF_GA + D_MODEL
RMS_EPS = 1e-6
GN_EPS = 64e-5
DECAY_SCALE = 0.6065306597126334

CHUNK = 64
PAIR = 2 * HEAD
N_PAIR = D_RWKV // PAIR
HALO = 8
TM_FFN = 512
TF_FFN = 256
TM_MIX = 512
CHUNK_GROUP = 4
GATE_COLS = 256
VMEM_LIMIT = 56 * 1024 * 1024


def _dot(a, b):
    return jnp.dot(a.astype(BF), b.astype(BF), preferred_element_type=F32)


def _dot_nt(a, b):
    return lax.dot_general(a.astype(BF), b.astype(BF), (((1,), (1,)), ((), ())),
                           preferred_element_type=F32)


def _dot_f32(a, b):
    return jnp.dot(a, b, preferred_element_type=F32)


def _split2(x):
    hi = x.astype(BF)
    lo = (x - hi.astype(F32)).astype(BF)
    return hi, lo


def _dot3(a, b):
    ah, al = _split2(a)
    bh, bl = _split2(b)
    return (jnp.dot(jnp.concatenate([ah, al], axis=1), jnp.concatenate([bh, bh], axis=0),
                    preferred_element_type=F32)
            + jnp.dot(ah, bl, preferred_element_type=F32))


def _rms_norm(x, gain):
    return x * lax.rsqrt(jnp.mean(x * x, axis=-1, keepdims=True) + RMS_EPS) * gain


def _ffn_body(*refs, final_norm, with_cast):
    refs = list(refs)
    x_ref, gain_ref, wg_ref, wu_ref, wd_ref = refs[:5]
    rest = refs[5:]
    fn_ref = rest.pop(0) if final_norm else None
    cast_in_ref = rest.pop(0) if with_cast else None
    o_ref = rest.pop(0)
    cast_out_ref = rest.pop(0) if with_cast else None
    act_ref, = rest
    if with_cast:
        cast_out_ref[...] = cast_in_ref[...].astype(BF)
    x = x_ref[...]
    h = _rms_norm(x, gain_ref[...])
    for c in range(D_FF // TF_FFN):
        sl = slice(c * TF_FFN, (c + 1) * TF_FFN)
        g = _dot_f32(h, wg_ref[:, sl])
        u = _dot_f32(h, wu_ref[:, sl])
        act_ref[:, sl] = g * jax.nn.sigmoid(g) * u
    y = x + 0.5 * _dot_f32(act_ref[...], wd_ref[...])
    if final_norm:
        y = _rms_norm(y, fn_ref[...])
    o_ref[...] = y


def _const_spec(shape):
    return pl.BlockSpec(shape, lambda *_: (0,) * len(shape), pipeline_mode=pl.Buffered(1))


def _ffn(x, gain, wg, wu, wd, final_gain=None, cast_to_bf16=None):
    m = x.shape[0]
    steps = m // TM_FFN
    final_norm = final_gain is not None
    with_cast = cast_to_bf16 is not None
    tok_spec = pl.BlockSpec((TM_FFN, D_MODEL), lambda i: (i, 0))
    in_specs = [tok_spec, _const_spec((1, D_MODEL)), _const_spec((D_MODEL, D_FF)),
                _const_spec((D_MODEL, D_FF)), _const_spec((D_FF, D_MODEL))]
    args = [x, gain, wg, wu, wd]
    out_specs = tok_spec
    out_shape = jax.ShapeDtypeStruct((m, D_MODEL), F32)
    if final_norm:
        in_specs.append(_const_spec((1, D_MODEL)))
        args.append(final_gain)
    if with_cast:
        rows, cols = cast_to_bf16.shape
        assert rows % steps == 0
        cast_spec = pl.BlockSpec((rows // steps, cols), lambda i: (i, 0))
        in_specs.append(cast_spec)
        args.append(cast_to_bf16)
        out_specs = (tok_spec, cast_spec)
        out_shape = (out_shape, jax.ShapeDtypeStruct((rows, cols), BF))
    return pl.pallas_call(
        functools.partial(_ffn_body, final_norm=final_norm, with_cast=with_cast),
        grid=(steps,),
        in_specs=in_specs,
        out_specs=out_specs,
        out_shape=out_shape,
        scratch_shapes=[pltpu.VMEM((TM_FFN, D_FF), F32)],
        compiler_params=pltpu.CompilerParams(
            dimension_semantics=("arbitrary",), vmem_limit_bytes=VMEM_LIMIT),
        input_output_aliases={0: 0} if final_norm else {},
        name="ffn_final" if final_norm else "ffn",
    )(*args)


def _pair_index():
    row = lax.broadcasted_iota(jnp.int32, (PAIR, PAIR), 0)
    lane = lax.broadcasted_iota(jnp.int32, (PAIR, PAIR), 1)
    return row, lane


def _interleave(*gens):
    live = list(gens)
    while live:
        for g in list(live):
            try:
                next(g)
            except StopIteration:
                live.remove(g)


def _tri_inverse_stages(l_bds, out):
    row, lane = _pair_index()
    zero = jnp.zeros((PAIR, PAIR), F32)
    eye = jnp.where(row == lane, 1.0, 0.0).astype(F32)

    def off(m):
        return ((row // (2 * m)) == (lane // (2 * m))) & ((row % (2 * m)) >= m) & ((lane % (2 * m)) < m)

    ts = [eye + jnp.where(off(1), l, zero) for l in l_bds]
    m = 2
    while m < CHUNK:
        mask = off(m)
        tl = [_dot(t, jnp.where(mask, l, zero)) for t, l in zip(ts, l_bds)]
        yield
        ts = [t + _dot(x, t) for t, x in zip(ts, tl)]
        yield
        m *= 2
    out.extend(ts)


def _wkv_tables_stages(chunks, tril, out):
    c = CHUNK
    head0 = lax.broadcasted_iota(jnp.int32, (c, PAIR), 1) < HEAD
    row, lane = _pair_index()
    t_idx = row % c
    s_idx = lane % c
    blockdiag = (row // c) == (lane // c)
    zero = jnp.zeros((c, PAIR), F32)
    zero2 = jnp.zeros((PAIR, PAIR), F32)

    e_tots, lhs0, lhs1, rhs0, rhs1, at_bd, rt_bd, bkh, v_swap = [], [], [], [], [], [], [], [], []
    for r, k, v, a, b, lw in chunks:
        h1 = lw.astype(BF)
        r1 = lw - h1.astype(F32)
        h2 = r1.astype(BF)
        h3 = (r1 - h2.astype(F32)).astype(BF)
        cs = jnp.dot(tril, jnp.concatenate([h1, h2, h3], axis=0), preferred_element_type=F32)
        tot = cs[c - 1:c, :]
        e_tots.append(jnp.exp(tot))
        at = a * jnp.exp(cs - lw)
        rt = r * jnp.exp(cs)
        e_n = jnp.exp(-cs)
        bt = b * e_n
        kt = k * e_n
        bh = bt * e_tots[-1]
        kh = kt * e_tots[-1]
        for j in range(N_PAIR):
            sl = slice(j * PAIR, (j + 1) * PAIR)
            ar0 = jnp.concatenate([jnp.where(head0, at[:, sl], zero), jnp.where(head0, rt[:, sl], zero)], axis=0)
            ar1 = jnp.concatenate([jnp.where(head0, zero, at[:, sl]), jnp.where(head0, zero, rt[:, sl])], axis=0)
            lhs0.append(ar0)
            lhs1.append(ar1)
            rhs0.append(jnp.concatenate([bt[:, sl], kt[:, sl]], axis=0))
            rhs1.append(jnp.concatenate([kt[:, sl], bt[:, sl]], axis=0))
            at_bd.append(jnp.concatenate([ar0[:c], ar1[:c]], axis=0))
            rt_bd.append(jnp.concatenate([ar0[c:], ar1[c:]], axis=0))
            bkh.append(jnp.concatenate([bh[:, sl], kh[:, sl]], axis=0))
            v_swap.append(jnp.concatenate([jnp.where(head0, zero, v[:, sl]),
                                           jnp.where(head0, v[:, sl], zero)], axis=0))
    yield
    g0 = [_dot_nt(x, y) for x, y in zip(lhs0, rhs0)]
    g1 = [_dot_nt(x, y) for x, y in zip(lhs1, rhs1)]
    yield
    aa = [jnp.where(s_idx < t_idx, jnp.concatenate([x[:c], y[:c]], axis=0), zero2) for x, y in zip(g0, g1)]
    arr = [jnp.where(s_idx <= t_idx, jnp.concatenate([x[c:], y[c:]], axis=0), zero2) for x, y in zip(g0, g1)]
    l_bd = [jnp.where(blockdiag, x, zero2) for x in aa]
    ak_ad = [jnp.where(blockdiag, zero2, x) for x in aa]
    akv = [_dot(x, y) for x, y in zip(ak_ad, v_swap)]
    yield
    t_inv = []
    yield from _tri_inverse_stages(l_bd, t_inv)
    t_at_akv = [_dot(t, jnp.concatenate([x, w], axis=1)) for t, x, w in zip(t_inv, at_bd, akv)]
    yield
    tar = [jnp.concatenate([z[:, :PAIR], y], axis=0) for z, y in zip(t_at_akv, rt_bd)]
    u_v = [z[:, PAIR:] for z in t_at_akv]
    for i in range(len(chunks)):
        sl = slice(i * N_PAIR, (i + 1) * N_PAIR)
        out.append(((tar[sl], u_v[sl], arr[sl], bkh[sl], v_swap[sl]), e_tots[i]))


def _wkv_apply_stages(tables, v, e_tot, states, out):
    c = CHUNK
    tar, u_v, arr, bkh, v_swap = tables
    head0 = lax.broadcasted_iota(jnp.int32, (c, PAIR), 1) < HEAD
    row, lane = _pair_index()
    blockdiag = (row // c) == (lane // c)
    zero2 = jnp.zeros((PAIR, PAIR), F32)
    xs = [_dot_nt(x, s) for x, s in zip(tar, states)]
    yield
    u_bd = [z[:PAIR] + w for z, w in zip(xs, u_v)]
    rs_bd = [z[PAIR:] for z in xs]
    y_bd = [_dot(m, u + w) + z for m, u, w, z in zip(arr, u_bd, v_swap, rs_bd)]
    ys = [jnp.where(head0, z[:c], z[c:]) for z in y_bd]
    new_states = []
    for j in range(N_PAIR):
        sl = slice(j * PAIR, (j + 1) * PAIR)
        uv = jnp.concatenate([u_bd[j][:c] + u_bd[j][c:], v[:, sl]], axis=0)
        upd = _dot(uv.T, bkh[j])
        new_states.append(jnp.where(blockdiag, states[j] * e_tot[:, sl] + upd, zero2))
    yield
    out.append((jnp.concatenate(ys, axis=1), new_states))


def _mixer_body(x_ref, gain_ref, win_ref, convw_ref, wouta_ref, mu_ref, w0_ref, wdec_ref,
                a0_ref, wicl_ref, wgate_ref, kk_ref, ka_ref, rk_ref, lnw_ref, lnb_ref,
                woutb_ref, wo_ref,
                o_ref,
                cu_buf, pb_buf, r_s, k_s, v_s, a_s, b_s, lw_s, g_s, bon_s, y_s, m_s, st_s):
    tm = TM_MIX

    @pl.when(pl.program_id(1) == 0)
    def _():
        cu_buf[0:HALO, :] = jnp.zeros((HALO, D_CONV), F32)
        pb_buf[0:HALO, :] = jnp.zeros((HALO, COLS_B), F32)
        st_s[...] = jnp.zeros_like(st_s)

    x = x_ref[...]
    h = _rms_norm(x, gain_ref[...]).astype(BF)

    def head_sum(z):
        n = z.shape[0]
        first = lax.broadcasted_iota(jnp.int32, (n, PAIR), 1) < HEAD
        outs = []
        for j in range(N_PAIR):
            t = z[:, j * PAIR:(j + 1) * PAIR]
            s0 = jnp.sum(jnp.where(first, t, 0.0), axis=-1, keepdims=True)
            s1 = jnp.sum(jnp.where(first, 0.0, t), axis=-1, keepdims=True)
            outs.append(jnp.where(first, s0, s1))
        return jnp.concatenate(outs, axis=1)

    def proj(lo, hi):
        return jnp.dot(h, win_ref[:, lo:hi], preferred_element_type=F32)

    pa_parts = []

    def branch_a_proj_stages():
        for p in range(COLS_A // D_CONV):
            pa_parts.append(proj(p * D_CONV, (p + 1) * D_CONV))
            yield

    def branch_b_prep_stages():
        pb = proj(OFF_B, OFF_GA)
        yield
        pb_buf[HALO:HALO + tm, :] = pb
        prev = pb_buf[pl.ds(HALO - 1, tm), :]
        pb_buf[0:HALO, :] = pb_buf[tm:tm + HALO, :]
        pbm = pb + (prev - pb) * mu_ref[...]
        r = pbm[:, 0:D_RWKV]
        k = pbm[:, D_RWKV:2 * D_RWKV]
        v = pbm[:, 2 * D_RWKV:3 * D_RWKV]
        xwa = pbm[:, 3 * D_RWKV:3 * D_RWKV + 128]
        xg = pbm[:, 3 * D_RWKV + 128:COLS_B]
        zeros_lora = jnp.zeros_like(wdec_ref[...])
        wdec = jnp.concatenate([wdec_ref[...], zeros_lora], axis=0)
        wicl = jnp.concatenate([zeros_lora, wicl_ref[...]], axis=0)
        z = w0_ref[...] + _dot3(jnp.tanh(xwa), wdec)
        lw_s[...] = -DECAY_SCALE * jax.nn.sigmoid(z)
        iclr = jax.nn.sigmoid(a0_ref[...] + _dot3(xwa, wicl))
        g_s[...] = _dot_f32(jax.nn.sigmoid(xg), wgate_ref[...])
        yield
        kk = k * kk_ref[...]
        kk = kk * lax.rsqrt(jnp.maximum(head_sum(kk * kk), 1e-24))
        k2 = k * (1.0 + (iclr - 1.0) * ka_ref[...])
        r_s[...] = r
        k_s[...] = k2
        v_s[...] = v
        a_s[...] = -kk
        b_s[...] = kk * iclr
        yield
        bon_s[...] = head_sum(r * k2 * rk_ref[...]) * v
        yield

    _interleave(branch_b_prep_stages(), branch_a_proj_stages())

    cu = pa_parts[1] * pa_parts[2]
    cu_buf[HALO:HALO + tm, :] = cu
    cw = convw_ref[...]
    conv = (cw[2:3] * cu + cw[1:2] * cu_buf[pl.ds(HALO - 1, tm), :]
            + cw[0:1] * cu_buf[pl.ds(HALO - 2, tm), :])
    cu_buf[0:HALO, :] = cu_buf[tm:tm + HALO, :]
    gated = pa_parts[0] * conv

    def branch_a_out_stages():
        for q in range(D_MODEL // GATE_COLS):
            cols = slice(q * GATE_COLS, (q + 1) * GATE_COLS)
            ya = _dot_f32(gated, wouta_ref[:, cols])
            ga = proj(OFF_GA + q * GATE_COLS, OFF_GA + (q + 1) * GATE_COLS)
            m_s[:, cols] = jax.nn.sigmoid(ga) * ya
            yield

    group_rows = CHUNK_GROUP * CHUNK
    n_group = tm // group_rows
    sgb_parts = [[] for _ in range(n_group)]

    def gate_b_stages(gi):
        rows = slice(gi * group_rows, (gi + 1) * group_rows)
        for q in range(D_MODEL // GATE_COLS):
            lo = OFF_GB + q * GATE_COLS
            gb = jnp.dot(h[rows], win_ref[:, lo:lo + GATE_COLS], preferred_element_type=F32)
            sgb_parts[gi].append(jax.nn.sigmoid(gb))
            yield

    def post_stages(gi):
        rows = slice(gi * group_rows, (gi + 1) * group_rows)
        y = y_s[rows, :]
        mean = head_sum(y) * (1.0 / HEAD)
        yield
        d = y - mean
        var = head_sum(d * d) * (1.0 / HEAD)
        yield
        yn = d * lax.rsqrt(var + GN_EPS) * lnw_ref[...] + lnb_ref[...] + bon_s[rows, :]
        yb = _dot_f32(yn * g_s[rows, :], woutb_ref[...])
        yield
        merged = m_s[rows, :] + jnp.concatenate(sgb_parts[gi], axis=1) * yb
        o_ref[rows, :] = x[rows] + _dot_f32(merged, wo_ref[...])
        yield

    tril = jnp.where(lax.broadcasted_iota(jnp.int32, (CHUNK, 3 * CHUNK), 1) % CHUNK
                     <= lax.broadcasted_iota(jnp.int32, (CHUNK, 3 * CHUNK), 0), 1.0, 0.0).astype(BF)

    def chunk_rows(ci):
        return slice(ci * CHUNK, (ci + 1) * CHUNK)

    group_tables = [[] for _ in range(n_group)]
    states = [[st_s[j] for j in range(N_PAIR)]]

    def tables_stages(gi):
        chunks = []
        for ci in range(gi * CHUNK_GROUP, (gi + 1) * CHUNK_GROUP):
            rows = chunk_rows(ci)
            chunks.append((r_s[rows, :], k_s[rows, :], v_s[rows, :], a_s[rows, :], b_s[rows, :],
                           lw_s[rows, :]))
        yield from _wkv_tables_stages(chunks, tril, group_tables[gi])

    def apply_stages(gi):
        for i in range(CHUNK_GROUP):
            rows = chunk_rows(gi * CHUNK_GROUP + i)
            tables, e_tot = group_tables[gi][i]
            res = []
            yield from _wkv_apply_stages(tables, v_s[rows, :], e_tot, states[0], res)
            y_s[rows, :] = res[0][0]
            states[0] = res[0][1]

    _interleave(tables_stages(0), branch_a_out_stages())
    for gi in range(n_group):
        stages = [apply_stages(gi), gate_b_stages(gi)]
        if gi + 1 < n_group:
            stages.insert(0, tables_stages(gi + 1))
        if gi > 0:
            stages.append(post_stages(gi - 1))
        _interleave(*stages)
    for j in range(N_PAIR):
        st_s[j] = states[0][j]
    _interleave(post_stages(n_group - 1))


def _mixer(x, n_batch, *consts):
    m = x.shape[0]
    tiles = m // n_batch // TM_MIX
    tok_spec = pl.BlockSpec((TM_MIX, D_MODEL), lambda bi, ti: (bi * tiles + ti, 0))
    vec = lambda n: pltpu.VMEM((TM_MIX, n), F32)
    return pl.pallas_call(
        _mixer_body,
        grid=(n_batch, tiles),
        in_specs=[tok_spec] + [_const_spec(c.shape) for c in consts],
        out_specs=tok_spec,
        out_shape=jax.ShapeDtypeStruct((m, D_MODEL), F32),
        scratch_shapes=[
            pltpu.VMEM((TM_MIX + HALO, D_CONV), F32),
            pltpu.VMEM((TM_MIX + HALO, COLS_B), F32),
            vec(D_RWKV), vec(D_RWKV), vec(D_RWKV), vec(D_RWKV), vec(D_RWKV), vec(D_RWKV),
            vec(D_RWKV), vec(D_RWKV), vec(D_RWKV),
            vec(D_MODEL),
            pltpu.VMEM((N_PAIR, PAIR, PAIR), F32),
        ],
        compiler_params=pltpu.CompilerParams(
            dimension_semantics=("arbitrary", "arbitrary"), vmem_limit_bytes=VMEM_LIMIT),
        input_output_aliases={0: 0},
        name="mixer",
    )(x, *consts)


def kernel(x, ffn1_norm, ffn1_w_gate, ffn1_w_up, ffn1_w_down, mix_norm, w_in, conv_w, w_out_a, mu_b, w0, w_decay_up, a0, w_iclr_up, w_gate_up, k_k, k_a, r_k, ln_x_w, ln_x_b, w_out_b, w_o, ffn2_norm, ffn2_w_gate, ffn2_w_up, ffn2_w_down, final_norm):
    n_batch, seq, d = x.shape
    assert d == D_MODEL and seq % TM_MIX == 0 and (n_batch * seq) % TM_FFN == 0
    assert ffn1_norm.shape[0] == 1, "single layer"
    row = lambda t: t.reshape(1, -1).astype(F32)
    xf = x.reshape(n_batch * seq, d)

    x1, w_in_bf = _ffn(xf, row(ffn1_norm[0]), ffn1_w_gate[0], ffn1_w_up[0], ffn1_w_down[0],
                       cast_to_bf16=w_in[0])

    x2 = _mixer(x1, n_batch, mix_norm, w_in_bf, conv_w[0], w_out_a[0], mu_b, w0, w_decay_up[0], a0,
                w_iclr_up[0], w_gate_up[0], k_k, k_a, r_k.reshape(1, D_RWKV), ln_x_w, ln_x_b,
                w_out_b[0], w_o[0])

    out = _ffn(x2, row(ffn2_norm[0]), ffn2_w_gate[0], ffn2_w_up[0], ffn2_w_down[0],
               final_gain=row(final_norm))
    return out.reshape(n_batch, seq, d)
```
